```python
import jax, jax.numpy as jnp
from jax import lax
import numpy as np

D_MODEL = 1024
BATCH = 16
SEQ = 2048
DEPTH = 1
DEC_BATCH = 8
DEC_SEQ = 64
PAST_LEN = 4096

CHUNK = 64
GLA_HEADS = 4
GLA_DK = 128
GLA_DV = 256
GLA_K_WIDTH = GLA_HEADS * GLA_DK
GLA_V_WIDTH = GLA_HEADS * GLA_DV
GLA_GATE_RANK = 16
GLA_TAU = 16.0
ATT_HEADS = 16
ATT_DH = 64
ATT_WIDTH = ATT_HEADS * ATT_DH
BAND_PAST_CHUNKS = 8
ATT_PAST = BAND_PAST_CHUNKS * CHUNK
REL_CLIP = 128
D_FF = 2816
EPS = 1e-6
NEG_INF = -1e30
PROJ_SPLITS = (GLA_K_WIDTH, GLA_K_WIDTH, GLA_V_WIDTH, GLA_V_WIDTH, GLA_GATE_RANK,
               ATT_WIDTH, ATT_WIDTH, ATT_WIDTH, D_MODEL, D_MODEL)
PROJ_WIDTH = sum(PROJ_SPLITS)

kernel_name = "gla_chunkband_macaron_stream_step"


def rms_norm(x, g):
    x32 = x.astype(jnp.float32)
    y = x32 * lax.rsqrt(jnp.mean(x32 * x32, axis=-1, keepdims=True) + EPS)
    return (y * g.astype(jnp.float32)).astype(x.dtype)


def swiglu_ffn(x, w_in, w_out):
    gate, up = jnp.split(x @ w_in, 2, axis=-1)
    return (jax.nn.silu(gate) * up) @ w_out


def split_projection(n, w_in):
    idx = np.cumsum(PROJ_SPLITS)[:-1].tolist()
    return jnp.split(n @ w_in, idx, axis=-1)


def gla_chunked(q, k, v, log_a, s0):
    B, T, H, dk = q.shape
    dv = v.shape[-1]
    L = min(CHUNK, T)
    N = T // L
    f32 = jnp.float32
    qc = q.astype(f32).reshape(B, N, L, H, dk)
    kc = k.astype(f32).reshape(B, N, L, H, dk)
    vc = v.astype(f32).reshape(B, N, L, H, dv)
    b = jnp.cumsum(log_a.astype(f32).reshape(B, N, L, H, dk), axis=2)
    b_last = b[:, :, -1]
    q_dec = qc * jnp.exp(b)
    k_inv = kc * jnp.exp(-b)
    k_end = kc * jnp.exp(b_last[:, :, None] - b)
    causal = jnp.tril(jnp.ones((L, L), dtype=bool))
    scores = jnp.where(causal, jnp.einsum('bnihd,bnjhd->bnhij', q_dec, k_inv), 0.0)
    o_intra = jnp.einsum('bnhij,bnjhv->bnihv', scores, vc)
    incr = jnp.einsum('bnjhd,bnjhv->bnhdv', k_end, vc)
    decay = jnp.exp(b_last)

    def step(s, inp):
        d, u = inp
        return d[..., None] * s + u, s

    s_final, s_before = lax.scan(step, s0.astype(f32),
                                 (jnp.moveaxis(decay, 1, 0), jnp.moveaxis(incr, 1, 0)))
    o_inter = jnp.einsum('bnihd,nbhdv->bnihv', q_dec, s_before)
    return (o_intra + o_inter).reshape(B, T, H, dv), s_final


def gla_branch(q_a, k_a, v_a, r_a, f_a, w_gla_gate, b_gla_gate, gla_norm, s0):
    B, T, _ = q_a.shape
    q = q_a.reshape(B, T, GLA_HEADS, GLA_DK) * (GLA_DK ** -0.5)
    k = k_a.reshape(B, T, GLA_HEADS, GLA_DK)
    v = v_a.reshape(B, T, GLA_HEADS, GLA_DV)
    log_a = jax.nn.log_sigmoid((f_a @ w_gla_gate + b_gla_gate).astype(jnp.float32)) / GLA_TAU
    log_a = log_a.reshape(B, T, GLA_HEADS, GLA_DK)
    o, s_new = gla_chunked(q, k, v, log_a, s0)
    o = o * lax.rsqrt(jnp.mean(o * o, axis=-1, keepdims=True) + EPS)
    o = o * gla_norm.astype(jnp.float32).reshape(GLA_HEADS, GLA_DV)
    o = o.reshape(B, T, GLA_V_WIDTH) * jax.nn.silu(r_a.astype(jnp.float32))
    return o.astype(q_a.dtype), s_new.astype(q_a.dtype)


def rel_bias(table, q_pos, k_pos):
    rel = jnp.clip(q_pos[:, None] - k_pos[None, :], -REL_CLIP, REL_CLIP) + REL_CLIP
    return table[:, rel].astype(jnp.float32)


def attend(q, k, v, bias, valid):
    s = jnp.einsum('bqhd,bkhd->bhqk', q, k).astype(jnp.float32) * (ATT_DH ** -0.5) + bias
    s = jnp.where(valid, s, NEG_INF)
    p = jax.nn.softmax(s, axis=-1).astype(v.dtype)
    return jnp.einsum('bhqk,bkhd->bqhd', p, v)


def band_attention_prompt(q, k, v, table):
    B, T, H, d = q.shape
    N = T // CHUNK
    band = ATT_PAST + CHUNK
    kp = jnp.pad(k, ((0, 0), (ATT_PAST, 0), (0, 0), (0, 0)))
    vp = jnp.pad(v, ((0, 0), (ATT_PAST, 0), (0, 0), (0, 0)))
    bias = rel_bias(table, jnp.arange(CHUNK) + ATT_PAST, jnp.arange(band))

    def one_chunk(c):
        start = c * CHUNK
        qc = lax.dynamic_slice_in_dim(q, start, CHUNK, axis=1)
        kc = lax.dynamic_slice_in_dim(kp, start, band, axis=1)
        vc = lax.dynamic_slice_in_dim(vp, start, band, axis=1)
        valid = (start - ATT_PAST + jnp.arange(band)) >= 0
        return attend(qc, kc, vc, bias, valid)

    out = lax.map(one_chunk, jnp.arange(N))
    return jnp.moveaxis(out, 0, 1).reshape(B, T, H, d)


def band_attention_step(q, k, v, cache_k, cache_v, table):
    T = q.shape[1]
    C = cache_k.shape[1]
    keys = jnp.concatenate([cache_k.astype(k.dtype), k], axis=1)
    vals = jnp.concatenate([cache_v.astype(v.dtype), v], axis=1)
    bias = rel_bias(table, C + jnp.arange(T), jnp.arange(C + T))
    valid = jnp.ones((C + T,), dtype=bool)
    return attend(q, keys, vals, bias, valid)


def encoder_layer(x, cache_k, cache_v, state_gla, norm_ffn1, w_ffn1_in, w_ffn1_out,
                  norm_mix, w_in, w_gla_gate, b_gla_gate, gla_norm, attn_rel_bias,
                  w_branch_gla, w_branch_att, w_out, norm_ffn2, w_ffn2_in, w_ffn2_out):
    B, T, _ = x.shape
    h = x + 0.5 * swiglu_ffn(rms_norm(x, norm_ffn1), w_ffn1_in, w_ffn1_out)
    n = rms_norm(h, norm_mix)
    q_a, k_a, v_a, r_a, f_a, q_b, k_b, v_b, g_a, g_b = split_projection(n, w_in)
    if state_gla is None:
        s0 = jnp.zeros((B, GLA_HEADS, GLA_DK, GLA_DV), jnp.float32)
    else:
        s0 = state_gla
    o_a, s_new = gla_branch(q_a, k_a, v_a, r_a, f_a, w_gla_gate, b_gla_gate, gla_norm, s0)
    qh = q_b.reshape(B, T, ATT_HEADS, ATT_DH)
    kh = k_b.reshape(B, T, ATT_HEADS, ATT_DH)
    vh = v_b.reshape(B, T, ATT_HEADS, ATT_DH)
    if cache_k is None:
        o_b = band_attention_prompt(qh, kh, vh, attn_rel_bias)
        k_keep, v_keep = kh[:, -ATT_PAST:], vh[:, -ATT_PAST:]
    else:
        o_b = band_attention_step(qh, kh, vh, cache_k, cache_v, attn_rel_bias)
        k_keep, v_keep = kh, vh
    mixed = (jax.nn.sigmoid(g_a) * (o_a @ w_branch_gla)
             + jax.nn.sigmoid(g_b) * (o_b.reshape(B, T, ATT_WIDTH) @ w_branch_att))
    h = h + mixed @ w_out
    h = h + 0.5 * swiglu_ffn(rms_norm(h, norm_ffn2), w_ffn2_in, w_ffn2_out)
    return h, k_keep, v_keep, s_new


def setup_inputs(seed: int = 0) -> dict:
    key = jax.random.key(seed)
    ks = jax.random.split(key, 24)
    f32 = jnp.float32

    def nrm(k, shape, scale):
        return jax.random.normal(k, shape, f32) * scale

    def gain(k, shape):
        return 1.0 + 0.05 * jax.random.normal(k, shape, f32)

    att_cache_len = min(ATT_PAST, PAST_LEN)
    return {
        "x_prompt": nrm(ks[0], (BATCH, SEQ, D_MODEL), 1.0),
        "x_sample": nrm(ks[1], (DEC_BATCH, DEC_SEQ, D_MODEL), 1.0),
        "cache_att_k": nrm(ks[2], (DEPTH, DEC_BATCH, att_cache_len, ATT_HEADS, ATT_DH), 1.0),
        "cache_att_v": nrm(ks[3], (DEPTH, DEC_BATCH, att_cache_len, ATT_HEADS, ATT_DH), 1.0),
        "state_gla": nrm(ks[4], (DEPTH, DEC_BATCH, GLA_HEADS, GLA_DK, GLA_DV), 0.5),
        "norm_ffn1": gain(ks[5], (DEPTH, D_MODEL)),
        "w_ffn1_in": nrm(ks[6], (DEPTH, D_MODEL, 2 * D_FF), D_MODEL ** -0.5),
        "w_ffn1_out": nrm(ks[7], (DEPTH, D_FF, D_MODEL), D_FF ** -0.5),
        "norm_mix": gain(ks[8], (DEPTH, D_MODEL)),
        "w_in": nrm(ks[9], (DEPTH, D_MODEL, PROJ_WIDTH), D_MODEL ** -0.5),
        "w_gla_gate": nrm(ks[10], (DEPTH, GLA_GATE_RANK, GLA_K_WIDTH), GLA_GATE_RANK ** -0.5),
        "b_gla_gate": nrm(ks[11], (DEPTH, GLA_K_WIDTH), 0.1),
        "gla_norm": gain(ks[12], (DEPTH, GLA_V_WIDTH)),
        "attn_rel_bias": nrm(ks[13], (DEPTH, ATT_HEADS, 2 * REL_CLIP + 1), 0.1),
        "w_branch_gla": nrm(ks[14], (DEPTH, GLA_V_WIDTH, D_MODEL), GLA_V_WIDTH ** -0.5),
        "w_branch_att": nrm(ks[15], (DEPTH, ATT_WIDTH, D_MODEL), ATT_WIDTH ** -0.5),
        "w_out": nrm(ks[16], (DEPTH, D_MODEL, D_MODEL), D_MODEL ** -0.5),
        "norm_ffn2": gain(ks[17], (DEPTH, D_MODEL)),
        "w_ffn2_in": nrm(ks[18], (DEPTH, D_MODEL, 2 * D_FF), D_MODEL ** -0.5),
        "w_ffn2_out": nrm(ks[19], (DEPTH, D_FF, D_MODEL), D_FF ** -0.5),
        "norm_final": gain(ks[20], (D_MODEL,)),
    }


def reference(x_prompt, x_sample, cache_att_k, cache_att_v, state_gla, norm_ffn1, w_ffn1_in,
              w_ffn1_out, norm_mix, w_in, w_gla_gate, b_gla_gate, gla_norm, attn_rel_bias,
              w_branch_gla, w_branch_att, w_out, norm_ffn2, w_ffn2_in, w_ffn2_out, norm_final):
    hp, hs = x_prompt, x_sample
    kp_list, vp_list, sp_list, ks_list, vs_list, ss_list = [], [], [], [], [], []
    for l in range(DEPTH):
        layer_w = (norm_ffn1[l], w_ffn1_in[l], w_ffn1_out[l], norm_mix[l], w_in[l],
                   w_gla_gate[l], b_gla_gate[l], gla_norm[l], attn_rel_bias[l],
                   w_branch_gla[l], w_branch_att[l], w_out[l], norm_ffn2[l],
                   w_ffn2_in[l], w_ffn2_out[l])
        hp, kp, vp, sp = encoder_layer(hp, None, None, None, *layer_w)
        hs, kn, vn, sn = encoder_layer(hs, cache_att_k[l], cache_att_v[l], state_gla[l], *layer_w)
        kp_list.append(kp); vp_list.append(vp); sp_list.append(sp)
        ks_list.append(kn); vs_list.append(vn); ss_list.append(sn)
    y_prompt = rms_norm(hp, norm_final)
    y_sample = rms_norm(hs, norm_final)
    new_att_k_prompt = jnp.stack(kp_list)
    new_att_v_prompt = jnp.stack(vp_list)
    new_gla_prompt = jnp.stack(sp_list)
    new_att_k_sample = jnp.stack(ks_list)
    new_att_v_sample = jnp.stack(vs_list)
    new_gla_sample = jnp.stack(ss_list)
    return (y_prompt, y_sample, new_att_k_prompt, new_att_v_prompt, new_gla_prompt,
            new_att_k_sample, new_att_v_sample, new_gla_sample)
```

```python
import functools

import jax
import jax.numpy as jnp
import numpy as np
from jax import lax
from jax.experimental import pallas as pl
from jax.experimental.pallas import tpu as pltpu

F32 = jnp.float32
BF16 = jnp.bfloat16

D_MODEL = 1024
CHUNK = 64
GLA_HEADS = 4
GLA_DK = 128
GLA_DV = 256
GLA_K_WIDTH = GLA_HEADS * GLA_DK
GLA_V_WIDTH = GLA_HEADS * GLA_DV
GLA_GATE_RANK = 16
GLA_TAU = 16.0
ATT_HEADS = 16
ATT_DH = 64
ATT_WIDTH = ATT_HEADS * ATT_DH
ATT_PAST = 8 * CHUNK
REL_CLIP = 128
D_FF = 2816
EPS = 1e-6
NEG_INF = -1e30

LANES = 128
VMEM_LIMIT = 56 * 1024 * 1024

COL_QA = 0
COL_KA = COL_QA + GLA_K_WIDTH
COL_VA = COL_KA + GLA_K_WIDTH
COL_RA = COL_VA + GLA_V_WIDTH
COL_QB = COL_RA + GLA_V_WIDTH
COL_KB = COL_QB + ATT_WIDTH
COL_VB = COL_KB + ATT_WIDTH
COL_GA = COL_VB + ATT_WIDTH
COL_GB = COL_GA + D_MODEL
COL_FA = COL_GB + D_MODEL
PROJ_COLS = COL_FA + LANES


def _const_spec(shape):
    nd = len(shape)
    return pl.BlockSpec(shape, lambda *_: (0,) * nd, pipeline_mode=pl.Buffered(1))


def _params(*sem):
    return pltpu.CompilerParams(dimension_semantics=sem, vmem_limit_bytes=VMEM_LIMIT)


def _rms(x, g):
    return x * lax.rsqrt(jnp.mean(x * x, axis=-1, keepdims=True) + EPS) * g


def _swiglu(xn, win_ref, wout_ref):
    gate = jnp.dot(xn, win_ref[:, :D_FF], preferred_element_type=F32)
    up = jnp.dot(xn, win_ref[:, D_FF:], preferred_element_type=F32)
    act = (gate * jax.nn.sigmoid(gate) * up).astype(BF16)
    return jnp.dot(act, wout_ref[...], preferred_element_type=F32)


def _ffn1_kernel(x_ref, g1_ref, win_ref, wout_ref, gmix_ref, h_ref, n_ref):
    x = x_ref[...]
    xn = _rms(x, g1_ref[...]).astype(BF16)
    h = x + 0.5 * _swiglu(xn, win_ref, wout_ref)
    h_ref[...] = h
    n_ref[...] = _rms(h, gmix_ref[...]).astype(BF16)


def _ffn1(x, g1, win, wout, gmix, tm):
    m = x.shape[0]
    row = lambda i: (i, 0)
    return pl.pallas_call(
        _ffn1_kernel,
        grid=(m // tm,),
        in_specs=[pl.BlockSpec((tm, D_MODEL), row), _const_spec(g1.shape), _const_spec(win.shape),
                  _const_spec(wout.shape), _const_spec(gmix.shape)],
        out_specs=[pl.BlockSpec((tm, D_MODEL), row), pl.BlockSpec((tm, D_MODEL), row)],
        out_shape=[jax.ShapeDtypeStruct((m, D_MODEL), F32), jax.ShapeDtypeStruct((m, D_MODEL), BF16)],
        compiler_params=_params("parallel"),
        name="ffn1",
    )(x, g1, win, wout, gmix)


PROJ_STEP = 1024


def _proj_kernel(n_ref, w_ref, o_ref):
    n = n_ref[...]
    for c0 in range(0, PROJ_COLS, PROJ_STEP):
        c1 = min(c0 + PROJ_STEP, PROJ_COLS)
        o_ref[:, c0:c1] = jnp.dot(n, w_ref[:, c0:c1], preferred_element_type=F32).astype(BF16)


def _proj(n, w, tm):
    m = n.shape[0]
    row = lambda i: (i, 0)
    return pl.pallas_call(
        _proj_kernel,
        grid=(m // tm,),
        in_specs=[pl.BlockSpec((tm, D_MODEL), row), _const_spec(w.shape)],
        out_specs=pl.BlockSpec((tm, PROJ_COLS), row),
        out_shape=jax.ShapeDtypeStruct((m, PROJ_COLS), BF16),
        compiler_params=_params("parallel"),
        name="proj",
    )(n, w)


def _gla_kernel(q_ref, k_ref, v_ref, r_ref, f_ref, wg_ref, bg_ref, gn_ref, s0_ref, o_ref, s_ref, *, n_chunks):
    nc, cl = n_chunks, CHUNK
    bmm = functools.partial(jnp.einsum, preferred_element_type=F32)

    z = jnp.dot(f_ref[...], wg_ref[...], preferred_element_type=F32) + bg_ref[...]
    log_a = (jnp.minimum(z, 0.0) - jnp.log1p(jnp.exp(-jnp.abs(z)))) / GLA_TAU
    log_a = log_a.reshape(nc, cl, GLA_DK)

    ri = lax.broadcasted_iota(jnp.int32, (cl, cl), 0)
    ci = lax.broadcasted_iota(jnp.int32, (cl, cl), 1)
    causal = ri >= ci
    tri = jnp.broadcast_to(causal.astype(BF16), (nc, cl, cl))
    hi = log_a.astype(BF16)
    lo = (log_a - hi.astype(F32)).astype(BF16)
    b = bmm('nij,njd->nid', tri, hi) + bmm('nij,njd->nid', tri, lo)
    b_last = b[:, cl - 1:cl, :]

    q = q_ref[...].astype(F32).reshape(nc, cl, GLA_DK) * (GLA_DK ** -0.5)
    k = k_ref[...].astype(F32).reshape(nc, cl, GLA_DK)
    v = v_ref[...].reshape(nc, cl, GLA_DV)
    q_dec = (q * jnp.exp(b)).astype(BF16)
    k_inv = (k * jnp.exp(-b)).astype(BF16)
    k_end = (k * jnp.exp(b_last - b)).astype(BF16)

    scores = jnp.where(causal, bmm('nid,njd->nij', q_dec, k_inv), 0.0)
    o_intra = bmm('nij,njv->niv', scores.astype(BF16), v)
    incr_t = bmm('njv,njd->nvd', v, k_end)
    decay = jnp.exp(b_last)

    s = s0_ref[...].T
    before = []
    for c in range(nc):
        before.append(s.astype(BF16))
        s = s * decay[c] + incr_t[c]
    s_ref[...] = s.T
    s_before = jnp.stack(before)
    o = o_intra + bmm('nid,nvd->niv', q_dec, s_before)

    o = o * lax.rsqrt(jnp.mean(o * o, axis=-1, keepdims=True) + EPS) * gn_ref[...]
    r = r_ref[...].astype(F32)
    o = o.reshape(nc * cl, GLA_DV) * (r * jax.nn.sigmoid(r))
    o_ref[...] = o.astype(BF16)


def _gla(p, wg, bg, gn, s0, batch, t):
    hq, hk = COL_QA // GLA_DK, COL_KA // GLA_DK
    hv, hr = COL_VA // GLA_DV, COL_RA // GLA_DV
    return pl.pallas_call(
        functools.partial(_gla_kernel, n_chunks=t // CHUNK),
        grid=(batch, GLA_HEADS),
        in_specs=[
            pl.BlockSpec((t, GLA_DK), lambda b, h: (b, hq + h)),
            pl.BlockSpec((t, GLA_DK), lambda b, h: (b, hk + h)),
            pl.BlockSpec((t, GLA_DV), lambda b, h: (b, hv + h)),
            pl.BlockSpec((t, GLA_DV), lambda b, h: (b, hr + h)),
            pl.BlockSpec((t, LANES), lambda b, h: (b, COL_FA // LANES)),
            pl.BlockSpec((LANES, GLA_DK), lambda b, h: (0, h)),
            pl.BlockSpec((1, GLA_DK), lambda b, h: (0, h)),
            pl.BlockSpec((1, GLA_DV), lambda b, h: (0, h)),
            pl.BlockSpec((None, None, GLA_DK, GLA_DV), lambda b, h: (b, h, 0, 0)),
        ],
        out_specs=[
            pl.BlockSpec((t, GLA_DV), lambda b, h: (b, h)),
            pl.BlockSpec((None, None, GLA_DK, GLA_DV), lambda b, h: (b, h, 0, 0)),
        ],
        out_shape=[jax.ShapeDtypeStruct((batch * t, GLA_V_WIDTH), BF16),
                   jax.ShapeDtypeStruct((batch, GLA_HEADS, GLA_DK, GLA_DV), F32)],
        compiler_params=_params("parallel", "parallel"),
        name="gla",
    )(p, p, p, p, p, wg, bg, gn, s0)


def _attend(q, k, v, bias_ref, boff):
    nk = k.shape[0]
    lane = lax.broadcasted_iota(jnp.int32, (1, 2 * ATT_DH), 1)
    outs = []
    for hh in range(2):
        mine = (lane >= hh * ATT_DH) & (lane < (hh + 1) * ATT_DH)
        qh = jnp.where(mine, q * (ATT_DH ** -0.5), 0.0).astype(BF16)
        s = lax.dot_general(qh, k, (((1,), (1,)), ((), ())), preferred_element_type=F32)
        s = s + bias_ref[hh, :, boff:boff + nk]
        e = jnp.exp(s - jnp.max(s, axis=-1, keepdims=True))
        pv = jnp.dot(e.astype(BF16), v, preferred_element_type=F32)
        outs.append(pv / jnp.sum(e, axis=-1, keepdims=True))
    return jnp.where(lane < ATT_DH, outs[0], outs[1]).astype(BF16)


def _band_prompt_kernel(q_ref, k_ref, v_ref, bias_ref, o_ref, *, tq):
    t = pl.program_id(2)
    n_past = ATT_PAST // tq

    for tt in range(n_past):
        @pl.when(t == tt)
        def _(tt=tt):
            nk = (tt + 1) * tq
            o_ref[...] = _attend(q_ref[...], k_ref[0:nk, :], v_ref[0:nk, :], bias_ref, ATT_PAST - tt * tq)

    @pl.when(t >= n_past)
    def _():
        start = pl.multiple_of((t - n_past) * tq, tq)
        nk = ATT_PAST + tq
        o_ref[...] = _attend(q_ref[...], k_ref[pl.ds(start, nk), :], v_ref[pl.ds(start, nk), :], bias_ref, 0)


def _band_prompt(p, bias, batch, t, tq):
    nt = t // tq
    w = 2 * ATT_DH
    cq, ck, cv = COL_QB // w, COL_KB // w, COL_VB // w
    return pl.pallas_call(
        functools.partial(_band_prompt_kernel, tq=tq),
        grid=(batch, ATT_HEADS // 2, nt),
        in_specs=[
            pl.BlockSpec((tq, w), lambda b, hp, i: (b * nt + i, cq + hp)),
            pl.BlockSpec((t, w), lambda b, hp, i: (b, ck + hp)),
            pl.BlockSpec((t, w), lambda b, hp, i: (b, cv + hp)),
            pl.BlockSpec((2, tq, ATT_PAST + tq), lambda b, hp, i: (hp, 0, 0)),
        ],
        out_specs=pl.BlockSpec((tq, w), lambda b, hp, i: (b * nt + i, hp)),
        out_shape=jax.ShapeDtypeStruct((batch * t, ATT_WIDTH), BF16),
        compiler_params=_params("parallel", "parallel", "arbitrary"),
        name="band_prompt",
    )(p, p, p, bias)


def _band_step_kernel(q_ref, k_ref, v_ref, ck_ref, cv_ref, bias_ref, o_ref):
    k = jnp.concatenate([ck_ref[...], k_ref[...]], axis=0)
    v = jnp.concatenate([cv_ref[...], v_ref[...]], axis=0)
    o_ref[...] = _attend(q_ref[...], k, v, bias_ref, 0)


def _band_step(p, cache_k, cache_v, bias, batch, t):
    w = 2 * ATT_DH
    c = cache_k.shape[1]
    cq, ck, cv = COL_QB // w, COL_KB // w, COL_VB // w
    return pl.pallas_call(
        _band_step_kernel,
        grid=(batch, ATT_HEADS // 2),
        in_specs=[
            pl.BlockSpec((t, w), lambda b, hp: (b, cq + hp)),
            pl.BlockSpec((t, w), lambda b, hp: (b, ck + hp)),
            pl.BlockSpec((t, w), lambda b, hp: (b, cv + hp)),
            pl.BlockSpec((None, c, w), lambda b, hp: (b, 0, hp)),
            pl.BlockSpec((None, c, w), lambda b, hp: (b, 0, hp)),
            pl.BlockSpec((2, t, c + t), lambda b, hp: (hp, 0, 0)),
        ],
        out_specs=pl.BlockSpec((t, w), lambda b, hp: (b, hp)),
        out_shape=jax.ShapeDtypeStruct((batch * t, ATT_WIDTH), BF16),
        compiler_params=_params("parallel", "parallel"),
        name="band_step",
    )(p, p, p, cache_k, cache_v, bias)


def _band_bias(table, tq):
    qi = np.arange(tq)[:, None]
    kj = np.arange(ATT_PAST + tq)[None, :]
    rel = np.clip(ATT_PAST + qi - kj, -REL_CLIP, REL_CLIP) + REL_CLIP
    qc = qi // CHUNK
    kc = (kj - ATT_PAST) // CHUNK
    visible = (kc <= qc) & (kc >= qc - ATT_PAST // CHUNK)
    return jnp.where(visible[None], table[:, rel].astype(F32), NEG_INF)


def _mix_ffn2_kernel(h_ref, oa_ref, ob_ref, ga_ref, gb_ref, wbg_ref, wba_ref, wo_ref, g2_ref,
                     win_ref, wout_ref, gf_ref, y_ref):
    a = jnp.dot(oa_ref[...], wbg_ref[...], preferred_element_type=F32)
    b = jnp.dot(ob_ref[...], wba_ref[...], preferred_element_type=F32)
    mixed = (jax.nn.sigmoid(ga_ref[...].astype(F32)) * a
             + jax.nn.sigmoid(gb_ref[...].astype(F32)) * b)
    h = h_ref[...] + jnp.dot(mixed.astype(BF16), wo_ref[...], preferred_element_type=F32)
    xn = _rms(h, g2_ref[...]).astype(BF16)
    h = h + 0.5 * _swiglu(xn, win_ref, wout_ref)
    y_ref[...] = _rms(h, gf_ref[...])


def _mix_ffn2(h, oa, ob, p, wbg, wba, wo, g2, win, wout, gf, tm):
    m = h.shape[0]
    row = lambda i: (i, 0)
    tile = pl.BlockSpec((tm, D_MODEL), row)
    return pl.pallas_call(
        _mix_ffn2_kernel,
        grid=(m // tm,),
        in_specs=[tile, tile, tile,
                  pl.BlockSpec((tm, D_MODEL), lambda i: (i, COL_GA // D_MODEL)),
                  pl.BlockSpec((tm, D_MODEL), lambda i: (i, COL_GB // D_MODEL)),
                  _const_spec(wbg.shape), _const_spec(wba.shape), _const_spec(wo.shape),
                  _const_spec(g2.shape), _const_spec(win.shape), _const_spec(wout.shape),
                  _const_spec(gf.shape)],
        out_specs=tile,
        out_shape=jax.ShapeDtypeStruct((m, D_MODEL), F32),
        compiler_params=_params("parallel"),
        name="mix_ffn2",
    )(h, oa, ob, p, p, wbg, wba, wo, g2, win, wout, gf)


def _layer(x, cache_k, cache_v, s0, w, tm, tq):
    batch, t, _ = x.shape
    m = batch * t
    h, n = _ffn1(x.reshape(m, D_MODEL), w["g1"], w["ffn1_in"], w["ffn1_out"], w["gmix"], tm)
    p = _proj(n, w["w_in"], tm)
    o_a, s_new = _gla(p, w["wg"], w["bg"], w["gn"], s0, batch, t)
    if cache_k is None:
        o_b = _band_prompt(p, _band_bias(w["table"], tq), batch, t, tq)
    else:
        o_b = _band_step(p, cache_k, cache_v, _band_bias(w["table"], t), batch, t)
    y = _mix_ffn2(h, o_a, o_b, p, w["wbg"], w["wba"], w["wo"], w["g2"], w["ffn2_in"], w["ffn2_out"],
                  w["gf"], tm)
    keep = min(ATT_PAST, t)
    p3 = p.reshape(batch, t, PROJ_COLS)
    k_keep = p3[:, t - keep:, COL_KB:COL_KB + ATT_WIDTH].astype(F32).reshape(1, batch, keep, ATT_HEADS, ATT_DH)
    v_keep = p3[:, t - keep:, COL_VB:COL_VB + ATT_WIDTH].astype(F32).reshape(1, batch, keep, ATT_HEADS, ATT_DH)
    return y.reshape(batch, t, D_MODEL), k_keep, v_keep, s_new[None]


def kernel(x_prompt, x_sample, cache_att_k, cache_att_v, state_gla, norm_ffn1, w_ffn1_in, w_ffn1_out,
           norm_mix, w_in, w_gla_gate, b_gla_gate, gla_norm, attn_rel_bias, w_branch_gla, w_branch_att,
           w_out, norm_ffn2, w_ffn2_in, w_ffn2_out, norm_final):
    assert norm_ffn1.shape[0] == 1, "single layer"
    wi = w_in[0]
    fa0 = COL_QB
    w_perm = jnp.concatenate(
        [wi[:, :fa0], wi[:, fa0 + GLA_GATE_RANK:], wi[:, fa0:fa0 + GLA_GATE_RANK],
         jnp.zeros((D_MODEL, LANES - GLA_GATE_RANK), wi.dtype)], axis=1).astype(BF16)
    wg = jnp.concatenate([w_gla_gate[0], jnp.zeros((LANES - GLA_GATE_RANK, GLA_K_WIDTH), F32)],
                         axis=0).astype(BF16)
    w = dict(
        g1=norm_ffn1, ffn1_in=w_ffn1_in[0].astype(BF16), ffn1_out=w_ffn1_out[0].astype(BF16),
        gmix=norm_mix, w_in=w_perm, wg=wg, bg=b_gla_gate, gn=gla_norm, table=attn_rel_bias[0],
        wbg=w_branch_gla[0].astype(BF16), wba=w_branch_att[0].astype(BF16), wo=w_out[0].astype(BF16),
        g2=norm_ffn2, ffn2_in=w_ffn2_in[0].astype(BF16), ffn2_out=w_ffn2_out[0].astype(BF16),
        gf=norm_final.reshape(1, D_MODEL),
    )
    bp = x_prompt.shape[0]
    s0_prompt = jnp.zeros((bp, GLA_HEADS, GLA_DK, GLA_DV), F32)
    yp, kp, vp, sp = _layer(x_prompt, None, None, s0_prompt, w, tm=512, tq=256)
    bs, c = cache_att_k.shape[1], cache_att_k.shape[2]
    ck = cache_att_k[0].astype(BF16).reshape(bs, c, ATT_WIDTH)
    cv = cache_att_v[0].astype(BF16).reshape(bs, c, ATT_WIDTH)
    ys, ks, vs, ss = _layer(x_sample, ck, cv, state_gla[0], w, tm=256, tq=None)
    return yp, ys, kp, vp, sp, ks, vs, ss
```

```python
import functools

import jax
import jax.numpy as jnp
import numpy as np
from jax import lax
from jax.experimental import pallas as pl
from jax.experimental.pallas import tpu as pltpu

F32 = jnp.float32
BF16 = jnp.bfloat16

D_MODEL = 1024
CHUNK = 64
GLA_HEADS = 4
GLA_DK = 128
GLA_DV = 256
GLA_K_WIDTH = GLA_HEADS * GLA_DK
GLA_V_WIDTH = GLA_HEADS * GLA_DV
GLA_GATE_RANK = 16
GLA_TAU = 16.0
ATT_HEADS = 16
ATT_DH = 64
ATT_WIDTH = ATT_HEADS * ATT_DH
ATT_PAST = 8 * CHUNK
REL_CLIP = 128
D_FF = 2816
EPS = 1e-6
NEG_INF = -1e30
LOG2E = 1.4426950408889634

LANES = 128
VMEM_LIMIT = 56 * 1024 * 1024

COL_QA = 0
COL_KA = COL_QA + GLA_K_WIDTH
COL_VA = COL_KA + GLA_K_WIDTH
COL_RA = COL_VA + GLA_V_WIDTH
COL_QB = COL_RA + GLA_V_WIDTH
COL_KB = COL_QB + ATT_WIDTH
COL_VB = COL_KB + ATT_WIDTH
COL_GA = COL_VB + ATT_WIDTH
COL_GB = COL_GA + D_MODEL
COL_FA = COL_GB + D_MODEL
PROJ_COLS = COL_FA + LANES


def _const_spec(shape):
    nd = len(shape)
    return pl.BlockSpec(shape, lambda *_: (0,) * nd, pipeline_mode=pl.Buffered(1))


def _params(*sem):
    return pltpu.CompilerParams(dimension_semantics=sem, vmem_limit_bytes=VMEM_LIMIT)


def _rms(x, g):
    return x * lax.rsqrt(jnp.mean(x * x, axis=-1, keepdims=True) + EPS) * g


def _swiglu(xn, win_ref, wout_ref):
    gate = jnp.dot(xn, win_ref[:, :D_FF], preferred_element_type=F32)
    up = jnp.dot(xn, win_ref[:, D_FF:], preferred_element_type=F32)
    act = (gate * jax.nn.sigmoid(gate) * up).astype(BF16)
    return jnp.dot(act, wout_ref[...], preferred_element_type=F32)


def _ffn1_kernel(x_ref, g1_ref, win_ref, wout_ref, gmix_ref, h_ref, n_ref):
    x = x_ref[...]
    xn = _rms(x, g1_ref[...]).astype(BF16)
    h = x + 0.5 * _swiglu(xn, win_ref, wout_ref)
    h_ref[...] = h
    n_ref[...] = _rms(h, gmix_ref[...]).astype(BF16)


def _ffn1(x, g1, win, wout, gmix, tm):
    m = x.shape[0]
    row = lambda i: (i, 0)
    return pl.pallas_call(
        _ffn1_kernel,
        grid=(m // tm,),
        in_specs=[pl.BlockSpec((tm, D_MODEL), row), _const_spec(g1.shape), _const_spec(win.shape),
                  _const_spec(wout.shape), _const_spec(gmix.shape)],
        out_specs=[pl.BlockSpec((tm, D_MODEL), row), pl.BlockSpec((tm, D_MODEL), row)],
        out_shape=[jax.ShapeDtypeStruct((m, D_MODEL), F32), jax.ShapeDtypeStruct((m, D_MODEL), BF16)],
        compiler_params=_params("parallel"),
        name="ffn1",
    )(x, g1, win, wout, gmix)


PROJ_STEP = 1024


def _proj_kernel(n_ref, w_ref, o_ref):
    n = n_ref[...]
    for c0 in range(0, PROJ_COLS, PROJ_STEP):
        c1 = min(c0 + PROJ_STEP, PROJ_COLS)
        o_ref[:, c0:c1] = jnp.dot(n, w_ref[:, c0:c1], preferred_element_type=F32).astype(BF16)


def _proj(n, w, tm):
    m = n.shape[0]
    row = lambda i: (i, 0)
    return pl.pallas_call(
        _proj_kernel,
        grid=(m // tm,),
        in_specs=[pl.BlockSpec((tm, D_MODEL), row), _const_spec(w.shape)],
        out_specs=pl.BlockSpec((tm, PROJ_COLS), row),
        out_shape=jax.ShapeDtypeStruct((m, PROJ_COLS), BF16),
        compiler_params=_params("parallel"),
        name="proj",
    )(n, w)


def _gla_kernel(q_ref, k_ref, v_ref, r_ref, f_ref, wg_ref, bg_ref, gn_ref, s0_ref, o_ref, s_ref, *, n_chunks):
    nc, cl = n_chunks, CHUNK
    bmm = functools.partial(jnp.einsum, preferred_element_type=F32)

    z = jnp.dot(f_ref[...], wg_ref[...], preferred_element_type=F32) + bg_ref[...]
    log_a = (jnp.minimum(z, 0.0) - jnp.log(1.0 + jnp.exp(-jnp.abs(z)))) * (LOG2E / GLA_TAU)
    log_a = log_a.reshape(nc, cl, GLA_DK)

    ri = lax.broadcasted_iota(jnp.int32, (cl, cl), 0)
    ci = lax.broadcasted_iota(jnp.int32, (cl, cl), 1)
    causal = ri >= ci
    tri = jnp.broadcast_to(causal.astype(BF16), (nc, cl, cl))
    hi = log_a.astype(BF16)
    lo = (log_a - hi.astype(F32)).astype(BF16)
    b = bmm('nij,njd->nid', tri, hi) + bmm('nij,njd->nid', tri, lo)
    b_last = b[:, cl - 1:cl, :]

    q = q_ref[...].astype(F32).reshape(nc, cl, GLA_DK) * (GLA_DK ** -0.5)
    k = k_ref[...].astype(F32).reshape(nc, cl, GLA_DK)
    v = v_ref[...].reshape(nc, cl, GLA_DV)
    q_dec = (q * jnp.exp2(b)).astype(BF16)
    k_inv = (k * jnp.exp2(-b)).astype(BF16)
    k_end = (k * jnp.exp2(b_last - b)).astype(BF16)

    scores = jnp.where(causal, bmm('nid,njd->nij', q_dec, k_inv), 0.0)
    o_intra = bmm('nij,njv->niv', scores.astype(BF16), v)
    incr_t = bmm('njv,njd->nvd', v, k_end)
    decay = jnp.exp2(b_last)

    s = s0_ref[...].T
    before = []
    for c in range(nc):
        before.append(s.astype(BF16))
        s = s * decay[c] + incr_t[c]
    s_ref[...] = s.T
    s_before = jnp.stack(before)
    o = o_intra + bmm('nid,nvd->niv', q_dec, s_before)

    o = o * lax.rsqrt(jnp.mean(o * o, axis=-1, keepdims=True) + EPS) * gn_ref[...]
    r = r_ref[...].astype(F32)
    o = o.reshape(nc * cl, GLA_DV) * (r * jax.nn.sigmoid(r))
    o_ref[...] = o.astype(BF16)


def _gla(p, wg, bg, gn, s0, batch, t):
    hq, hk = COL_QA // GLA_DK, COL_KA // GLA_DK
    hv, hr = COL_VA // GLA_DV, COL_RA // GLA_DV
    return pl.pallas_call(
        functools.partial(_gla_kernel, n_chunks=t // CHUNK),
        grid=(batch, GLA_HEADS),
        in_specs=[
            pl.BlockSpec((t, GLA_DK), lambda b, h: (b, hq + h)),
            pl.BlockSpec((t, GLA_DK), lambda b, h: (b, hk + h)),
            pl.BlockSpec((t, GLA_DV), lambda b, h: (b, hv + h)),
            pl.BlockSpec((t, GLA_DV), lambda b, h: (b, hr + h)),
            pl.BlockSpec((t, LANES), lambda b, h: (b, COL_FA // LANES)),
            pl.BlockSpec((LANES, GLA_DK), lambda b, h: (0, h)),
            pl.BlockSpec((1, GLA_DK), lambda b, h: (0, h)),
            pl.BlockSpec((1, GLA_DV), lambda b, h: (0, h)),
            pl.BlockSpec((None, None, GLA_DK, GLA_DV), lambda b, h: (b, h, 0, 0)),
        ],
        out_specs=[
            pl.BlockSpec((t, GLA_DV), lambda b, h: (b, h)),
            pl.BlockSpec((None, None, GLA_DK, GLA_DV), lambda b, h: (b, h, 0, 0)),
        ],
        out_shape=[jax.ShapeDtypeStruct((batch * t, GLA_V_WIDTH), BF16),
                   jax.ShapeDtypeStruct((batch, GLA_HEADS, GLA_DK, GLA_DV), F32)],
        compiler_params=_params("parallel", "parallel"),
        name="gla",
    )(p, p, p, p, p, wg, bg, gn, s0)


ATT_GROUP = 4
ATT_LANES = ATT_GROUP * ATT_DH
ATT_TQ = 256
REL_RING = 1024


def _rel_row(table):
    u = np.arange(REL_RING)
    u = np.where(u < ATT_PAST + ATT_TQ, u, u - REL_RING)
    idx = np.clip(ATT_PAST - u, -REL_CLIP, REL_CLIP) + REL_CLIP
    return table[:, idx].astype(F32)[:, None, :]


def _build_bias(rel_ref, bias_ref):
    _, tq, nk = bias_ref.shape
    qc = lax.broadcasted_iota(jnp.int32, (tq, nk), 0) // CHUNK
    kc = lax.broadcasted_iota(jnp.int32, (tq, nk), 1) // CHUNK
    visible = (kc >= qc) & (kc <= qc + ATT_PAST // CHUNK)
    for hh in range(ATT_GROUP):
        row = jnp.broadcast_to(rel_ref[hh], (tq, REL_RING))
        toeplitz = pltpu.roll(row, 0, 1, stride=1, stride_axis=0)
        bias_ref[hh] = jnp.where(visible, toeplitz[:, :nk] * LOG2E, NEG_INF)


def _head_lanes(hh):
    lane = lax.broadcasted_iota(jnp.int32, (1, ATT_LANES), 1)
    return (lane >= hh * ATT_DH) & (lane < (hh + 1) * ATT_DH)


def _head_values(v):
    return [jnp.where(_head_lanes(hh), v, jnp.ones_like(v)) for hh in range(ATT_GROUP)]


def _attend(q, k, vhs, bias_ref, boff):
    nk = k.shape[0]
    q = q.astype(F32) * (ATT_DH ** -0.5 * LOG2E)
    out = None
    for hh in range(ATT_GROUP):
        mine = _head_lanes(hh)
        qh = jnp.where(mine, q, 0.0).astype(BF16)
        s = lax.dot_general(qh, k, (((1,), (1,)), ((), ())), preferred_element_type=F32)
        s = s + bias_ref[hh, :, boff:boff + nk]
        e = jnp.exp2(s - jnp.max(s, axis=-1, keepdims=True)).astype(BF16)
        pv = jnp.dot(e, vhs[hh], preferred_element_type=F32)
        o = pv / pltpu.roll(pv, ATT_DH, 1)
        out = o if out is None else jnp.where(mine, o, out)
    return out.astype(BF16)


def _band_prompt_kernel(rel_ref, q_ref, k_ref, v_ref, o_ref, bias_ref):
    _, tq, nk = bias_ref.shape
    n_blocks = q_ref.shape[0] // tq
    n_past = ATT_PAST // tq

    @pl.when(pl.program_id(1) == 0)
    def _():
        _build_bias(rel_ref, bias_ref)

    vhs = _head_values(v_ref[...])
    for tt in range(n_blocks):
        rows = slice(tt * tq, (tt + 1) * tq)
        keys = slice(max(tt - n_past, 0) * tq, (tt + 1) * tq)
        o_ref[rows, :] = _attend(q_ref[rows, :], k_ref[keys, :], [vh[keys, :] for vh in vhs], bias_ref,
                                 max(n_past - tt, 0) * tq)


def _band_prompt(p, rel, batch, t):
    w = ATT_LANES
    cq, ck, cv = COL_QB // w, COL_KB // w, COL_VB // w
    return pl.pallas_call(
        _band_prompt_kernel,
        grid=(ATT_HEADS // ATT_GROUP, batch),
        in_specs=[
            pl.BlockSpec((ATT_GROUP, 1, REL_RING), lambda g, b: (g, 0, 0)),
            pl.BlockSpec((t, w), lambda g, b: (b, cq + g)),
            pl.BlockSpec((t, w), lambda g, b: (b, ck + g)),
            pl.BlockSpec((t, w), lambda g, b: (b, cv + g)),
        ],
        out_specs=pl.BlockSpec((t, w), lambda g, b: (b, g)),
        out_shape=jax.ShapeDtypeStruct((batch * t, ATT_WIDTH), BF16),
        scratch_shapes=[pltpu.VMEM((ATT_GROUP, ATT_TQ, ATT_PAST + ATT_TQ), F32)],
        compiler_params=_params("arbitrary", "arbitrary"),
        name="band_prompt",
    )(rel, p, p, p)


def _band_step_kernel(rel_ref, q_ref, k_ref, v_ref, ck_ref, cv_ref, o_ref, bias_ref):
    @pl.when(pl.program_id(1) == 0)
    def _():
        _build_bias(rel_ref, bias_ref)

    k = jnp.concatenate([ck_ref[...], k_ref[...]], axis=0)
    v = jnp.concatenate([cv_ref[...], v_ref[...]], axis=0)
    o_ref[...] = _attend(q_ref[...], k, _head_values(v), bias_ref, 0)


def _band_step(p, cache_k, cache_v, rel, batch, t):
    w = ATT_LANES
    c = cache_k.shape[1]
    cq, ck, cv = COL_QB // w, COL_KB // w, COL_VB // w
    return pl.pallas_call(
        _band_step_kernel,
        grid=(ATT_HEADS // ATT_GROUP, batch),
        in_specs=[
            pl.BlockSpec((ATT_GROUP, 1, REL_RING), lambda g, b: (g, 0, 0)),
            pl.BlockSpec((t, w), lambda g, b: (b, cq + g)),
            pl.BlockSpec((t, w), lambda g, b: (b, ck + g)),
            pl.BlockSpec((t, w), lambda g, b: (b, cv + g)),
            pl.BlockSpec((None, c, w), lambda g, b: (b, 0, g)),
            pl.BlockSpec((None, c, w), lambda g, b: (b, 0, g)),
        ],
        out_specs=pl.BlockSpec((t, w), lambda g, b: (b, g)),
        out_shape=jax.ShapeDtypeStruct((batch * t, ATT_WIDTH), BF16),
        scratch_shapes=[pltpu.VMEM((ATT_GROUP, t, c + t), F32)],
        compiler_params=_params("arbitrary", "arbitrary"),
        name="band_step",
    )(rel, p, p, p, cache_k, cache_v)


def _mix_ffn2_kernel(h_ref, oa_ref, ob_ref, ga_ref, gb_ref, wbg_ref, wba_ref, wo_ref, g2_ref,
                     win_ref, wout_ref, gf_ref, y_ref):
    a = jnp.dot(oa_ref[...], wbg_ref[...], preferred_element_type=F32)
    b = jnp.dot(ob_ref[...], wba_ref[...], preferred_element_type=F32)
    mixed = (jax.nn.sigmoid(ga_ref[...].astype(F32)) * a
             + jax.nn.sigmoid(gb_ref[...].astype(F32)) * b)
    h = h_ref[...] + jnp.dot(mixed.astype(BF16), wo_ref[...], preferred_element_type=F32)
    xn = _rms(h, g2_ref[...]).astype(BF16)
    h = h + 0.5 * _swiglu(xn, win_ref, wout_ref)
    y_ref[...] = _rms(h, gf_ref[...])


def _mix_ffn2(h, oa, ob, p, wbg, wba, wo, g2, win, wout, gf, tm):
    m = h.shape[0]
    row = lambda i: (i, 0)
    tile = pl.BlockSpec((tm, D_MODEL), row)
    return pl.pallas_call(
        _mix_ffn2_kernel,
        grid=(m // tm,),
        in_specs=[tile, tile, tile,
                  pl.BlockSpec((tm, D_MODEL), lambda i: (i, COL_GA // D_MODEL)),
                  pl.BlockSpec((tm, D_MODEL), lambda i: (i, COL_GB // D_MODEL)),
                  _const_spec(wbg.shape), _const_spec(wba.shape), _const_spec(wo.shape),
                  _const_spec(g2.shape), _const_spec(win.shape), _const_spec(wout.shape),
                  _const_spec(gf.shape)],
        out_specs=tile,
        out_shape=jax.ShapeDtypeStruct((m, D_MODEL), F32),
        compiler_params=_params("parallel"),
        name="mix_ffn2",
    )(h, oa, ob, p, p, wbg, wba, wo, g2, win, wout, gf)


def _layer(x, cache_k, cache_v, s0, w, tm):
    batch, t, _ = x.shape
    m = batch * t
    h, n = _ffn1(x.reshape(m, D_MODEL), w["g1"], w["ffn1_in"], w["ffn1_out"], w["gmix"], tm)
    p = _proj(n, w["w_in"], tm)
    o_a, s_new = _gla(p, w["wg"], w["bg"], w["gn"], s0, batch, t)
    if cache_k is None:
        o_b = _band_prompt(p, w["rel"], batch, t)
    else:
        assert cache_k.shape[1] == ATT_PAST and t <= ATT_TQ
        o_b = _band_step(p, cache_k, cache_v, w["rel"], batch, t)
    y = _mix_ffn2(h, o_a, o_b, p, w["wbg"], w["wba"], w["wo"], w["g2"], w["ffn2_in"], w["ffn2_out"],
                  w["gf"], tm)
    keep = min(ATT_PAST, t)
    p3 = p.reshape(batch, t, PROJ_COLS)
    k_keep = p3[:, t - keep:, COL_KB:COL_KB + ATT_WIDTH].astype(F32).reshape(1, batch, keep, ATT_HEADS, ATT_DH)
    v_keep = p3[:, t - keep:, COL_VB:COL_VB + ATT_WIDTH].astype(F32).reshape(1, batch, keep, ATT_HEADS, ATT_DH)
    return y.reshape(batch, t, D_MODEL), k_keep, v_keep, s_new[None]


def kernel(x_prompt, x_sample, cache_att_k, cache_att_v, state_gla, norm_ffn1, w_ffn1_in, w_ffn1_out,
           norm_mix, w_in, w_gla_gate, b_gla_gate, gla_norm, attn_rel_bias, w_branch_gla, w_branch_att,
           w_out, norm_ffn2, w_ffn2_in, w_ffn2_out, norm_final):
    assert norm_ffn1.shape[0] == 1, "single layer"
    wi = w_in[0]
    fa0 = COL_QB
    w_perm = jnp.concatenate(
        [wi[:, :fa0], wi[:, fa0 + GLA_GATE_RANK:], wi[:, fa0:fa0 + GLA_GATE_RANK],
         jnp.zeros((D_MODEL, LANES - GLA_GATE_RANK), wi.dtype)], axis=1).astype(BF16)
    wg = jnp.concatenate([w_gla_gate[0], jnp.zeros((LANES - GLA_GATE_RANK, GLA_K_WIDTH), F32)],
                         axis=0).astype(BF16)
    w = dict(
        g1=norm_ffn1, ffn1_in=w_ffn1_in[0].astype(BF16), ffn1_out=w_ffn1_out[0].astype(BF16),
        gmix=norm_mix, w_in=w_perm, wg=wg, bg=b_gla_gate, gn=gla_norm, rel=_rel_row(attn_rel_bias[0]),
        wbg=w_branch_gla[0].astype(BF16), wba=w_branch_att[0].astype(BF16), wo=w_out[0].astype(BF16),
        g2=norm_ffn2, ffn2_in=w_ffn2_in[0].astype(BF16), ffn2_out=w_ffn2_out[0].astype(BF16),
        gf=norm_final.reshape(1, D_MODEL),
    )
    bp = x_prompt.shape[0]
    s0_prompt = jnp.zeros((bp, GLA_HEADS, GLA_DK, GLA_DV), F32)
    yp, kp, vp, sp = _layer(x_prompt, None, None, s0_prompt, w, tm=512)
    bs, c = cache_att_k.shape[1], cache_att_k.shape[2]
    ck = cache_att_k[0].astype(BF16).reshape(bs, c, ATT_WIDTH)
    cv = cache_att_v[0].astype(BF16).reshape(bs, c, ATT_WIDTH)
    ys, ks, vs, ss = _layer(x_sample, ck, cv, state_gla[0], w, tm=256)
    return yp, ys, kp, vp, sp, ks, vs, ss
```

```python
import functools

import jax
import jax.numpy as jnp
import numpy as np
from jax import lax
from jax.experimental import pallas as pl
from jax.experimental.pallas import tpu as pltpu

F32 = jnp.float32
BF16 = jnp.bfloat16

D_MODEL = 1024
CHUNK = 64
GLA_HEADS = 4
GLA_DK = 128
GLA_DV = 256
GLA_K_WIDTH = GLA_HEADS * GLA_DK
GLA_V_WIDTH = GLA_HEADS * GLA_DV
GLA_GATE_RANK = 16
GLA_TAU = 16.0
ATT_HEADS = 16
ATT_DH = 64
ATT_WIDTH = ATT_HEADS * ATT_DH
ATT_PAST = 8 * CHUNK
REL_CLIP = 128
D_FF = 2816
EPS = 1e-6
NEG_INF = -1e30
LOG2E = 1.4426950408889634

LANES = 128
VMEM_LIMIT = 60 * 1024 * 1024

COL_QA = 0
COL_KA = COL_QA + GLA_K_WIDTH
COL_VA = COL_KA + GLA_K_WIDTH
COL_RA = COL_VA + GLA_V_WIDTH
COL_QB = COL_RA + GLA_V_WIDTH
COL_KB = COL_QB + ATT_WIDTH
COL_VB = COL_KB + ATT_WIDTH
COL_GA = COL_VB + ATT_WIDTH
COL_GB = COL_GA + D_MODEL
COL_FA = COL_GB + D_MODEL
PROJ_COLS = COL_FA + LANES


def _const_spec(shape):
    nd = len(shape)
    return pl.BlockSpec(shape, lambda *_: (0,) * nd, pipeline_mode=pl.Buffered(1))


def _params(*sem):
    return pltpu.CompilerParams(dimension_semantics=sem, vmem_limit_bytes=VMEM_LIMIT)


def _rms(x, g):
    return x * lax.rsqrt(jnp.mean(x * x, axis=-1, keepdims=True) + EPS) * g


def _swiglu(xn, win_ref, wout_ref):
    gate = jnp.dot(xn, win_ref[:, :D_FF], preferred_element_type=F32)
    up = jnp.dot(xn, win_ref[:, D_FF:], preferred_element_type=F32)
    act = (gate * jax.nn.sigmoid(gate) * up).astype(BF16)
    return jnp.dot(act, wout_ref[...], preferred_element_type=F32)


W_STEPS = 11


def _slab_spec(shape, axis, steps):
    block = tuple(d // steps if a == axis else d for a, d in enumerate(shape))
    assert block[axis] * steps == shape[axis] and steps <= W_STEPS
    if axis == 0:
        return pl.BlockSpec(block, lambda i: (jnp.minimum(i, steps - 1), 0))
    return pl.BlockSpec(block, lambda i: (0, jnp.minimum(i, steps - 1)))


def _stage(step, src_ref, dst_ref, axis):
    n = src_ref.shape[axis]
    for c in range(dst_ref.shape[axis] // n):
        @pl.when(step == c)
        def _(c=c):
            if axis == 0:
                dst_ref[c * n:(c + 1) * n, :] = src_ref[...].astype(BF16)
            else:
                dst_ref[:, c * n:(c + 1) * n] = src_ref[...].astype(BF16)


def _row_tile(tm):
    return pl.BlockSpec((tm, D_MODEL), lambda i: (jnp.maximum(i - W_STEPS, 0), 0))


def _ffn1_kernel(x_ref, g1_ref, win_ref, wout_ref, gmix_ref, h_ref, n_ref, win_s, wout_s):
    step = pl.program_id(0)
    _stage(step, win_ref, win_s, 1)
    _stage(step, wout_ref, wout_s, 0)

    @pl.when(step >= W_STEPS)
    def _():
        x = x_ref[...]
        xn = _rms(x, g1_ref[...]).astype(BF16)
        h = x + 0.5 * _swiglu(xn, win_s, wout_s)
        h_ref[...] = h
        n_ref[...] = _rms(h, gmix_ref[...]).astype(BF16)


def _ffn1(x, g1, win, wout, gmix, tm):
    m = x.shape[0]
    return pl.pallas_call(
        _ffn1_kernel,
        grid=(W_STEPS + m // tm,),
        in_specs=[_row_tile(tm), _const_spec(g1.shape), _slab_spec(win.shape, 1, W_STEPS),
                  _slab_spec(wout.shape, 0, W_STEPS), _const_spec(gmix.shape)],
        out_specs=[_row_tile(tm), _row_tile(tm)],
        out_shape=[jax.ShapeDtypeStruct((m, D_MODEL), F32), jax.ShapeDtypeStruct((m, D_MODEL), BF16)],
        scratch_shapes=[pltpu.VMEM(win.shape, BF16), pltpu.VMEM(wout.shape, BF16)],
        compiler_params=_params("arbitrary"),
        name="ffn1",
    )(x, g1, win, wout, gmix)


PROJ_STEP = 1024


def _proj_kernel(n_ref, w_ref, o_ref):
    n = n_ref[...]
    for c0 in range(0, PROJ_COLS, PROJ_STEP):
        c1 = min(c0 + PROJ_STEP, PROJ_COLS)
        o_ref[:, c0:c1] = jnp.dot(n, w_ref[:, c0:c1], preferred_element_type=F32).astype(BF16)


def _proj(n, w, tm):
    m = n.shape[0]
    row = lambda i: (i, 0)
    return pl.pallas_call(
        _proj_kernel,
        grid=(m // tm,),
        in_specs=[pl.BlockSpec((tm, D_MODEL), row), _const_spec(w.shape)],
        out_specs=pl.BlockSpec((tm, PROJ_COLS), row),
        out_shape=jax.ShapeDtypeStruct((m, PROJ_COLS), BF16),
        compiler_params=_params("parallel"),
        name="proj",
    )(n, w)


GLA_ROWS = 2048


def _gla_kernel(q_ref, k_ref, v_ref, r_ref, f_ref, wg_ref, bg_ref, gn_ref, s0_ref, o_ref, s_ref, *, n_seq):
    cl = CHUNK
    nc = q_ref.shape[0] // cl
    per_seq = nc // n_seq
    bmm = functools.partial(jnp.einsum, preferred_element_type=F32)

    z = jnp.dot(f_ref[...], wg_ref[...], preferred_element_type=F32) + bg_ref[...]
    log_a = (jnp.minimum(z, 0.0) - jnp.log(1.0 + jnp.exp(-jnp.abs(z)))) * (LOG2E / GLA_TAU)
    log_a = log_a.reshape(nc, cl, GLA_DK)

    ri = lax.broadcasted_iota(jnp.int32, (cl, cl), 0)
    ci = lax.broadcasted_iota(jnp.int32, (cl, cl), 1)
    causal = ri >= ci
    tri = jnp.broadcast_to(causal.astype(BF16), (nc, cl, cl))
    hi = log_a.astype(BF16)
    lo = (log_a - hi.astype(F32)).astype(BF16)
    b = bmm('nij,njd->nid', tri, hi) + bmm('nij,njd->nid', tri, lo)
    b_last = b[:, cl - 1:cl, :]

    q = q_ref[...].astype(F32).reshape(nc, cl, GLA_DK) * (GLA_DK ** -0.5)
    k = k_ref[...].astype(F32).reshape(nc, cl, GLA_DK)
    v = v_ref[...].reshape(nc, cl, GLA_DV)
    q_dec = (q * jnp.exp2(b)).astype(BF16)
    k_inv = (k * jnp.exp2(-b)).astype(BF16)
    k_end = (k * jnp.exp2(b_last - b)).astype(BF16)

    scores = jnp.where(causal, bmm('nid,njd->nij', q_dec, k_inv), 0.0)
    o_intra = bmm('nij,njv->niv', scores.astype(BF16), v)
    incr_t = bmm('njv,njd->nvd', v, k_end)
    decay = jnp.exp2(b_last)

    before = []
    for i in range(n_seq):
        s = s0_ref[i].T
        for c in range(i * per_seq, (i + 1) * per_seq):
            before.append(s.astype(BF16))
            s = s * decay[c] + incr_t[c]
        s_ref[i] = s.T
    s_before = jnp.stack(before)
    o = o_intra + bmm('nid,nvd->niv', q_dec, s_before)

    o = o * lax.rsqrt(jnp.mean(o * o, axis=-1, keepdims=True) + EPS) * gn_ref[...]
    r = r_ref[...].astype(F32)
    o = o.reshape(nc * cl, GLA_DV) * (r * jax.nn.sigmoid(r))
    o_ref[...] = o.astype(BF16)


def _gla(p, wg, bg, gn, s0, batch, t):
    hq, hk = COL_QA // GLA_DK, COL_KA // GLA_DK
    hv, hr = COL_VA // GLA_DV, COL_RA // GLA_DV
    n_seq = min(batch, max(1, GLA_ROWS // t))
    rows = n_seq * t
    state = pl.BlockSpec((n_seq, None, GLA_DK, GLA_DV), lambda b, h: (b, h, 0, 0))
    return pl.pallas_call(
        functools.partial(_gla_kernel, n_seq=n_seq),
        grid=(batch // n_seq, GLA_HEADS),
        in_specs=[
            pl.BlockSpec((rows, GLA_DK), lambda b, h: (b, hq + h)),
            pl.BlockSpec((rows, GLA_DK), lambda b, h: (b, hk + h)),
            pl.BlockSpec((rows, GLA_DV), lambda b, h: (b, hv + h)),
            pl.BlockSpec((rows, GLA_DV), lambda b, h: (b, hr + h)),
            pl.BlockSpec((rows, LANES), lambda b, h: (b, COL_FA // LANES)),
            pl.BlockSpec((LANES, GLA_DK), lambda b, h: (0, h)),
            pl.BlockSpec((1, GLA_DK), lambda b, h: (0, h)),
            pl.BlockSpec((1, GLA_DV), lambda b, h: (0, h)),
            state,
        ],
        out_specs=[pl.BlockSpec((rows, GLA_DV), lambda b, h: (b, h)), state],
        out_shape=[jax.ShapeDtypeStruct((batch * t, GLA_V_WIDTH), BF16),
                   jax.ShapeDtypeStruct((batch, GLA_HEADS, GLA_DK, GLA_DV), F32)],
        compiler_params=_params("parallel", "parallel"),
        name="gla",
    )(p, p, p, p, p, wg, bg, gn, s0)


ATT_GROUP = 4
ATT_LANES = ATT_GROUP * ATT_DH
ATT_TQ = 256
REL_RING = 1024


def _rel_row(table):
    u = np.arange(REL_RING)
    u = np.where(u < ATT_PAST + ATT_TQ, u, u - REL_RING)
    idx = np.clip(ATT_PAST - u, -REL_CLIP, REL_CLIP) + REL_CLIP
    return table[:, idx].astype(F32)[:, None, :]


def _build_bias(rel_ref, bias_ref):
    _, tq, nk = bias_ref.shape
    qc = lax.broadcasted_iota(jnp.int32, (tq, nk), 0) // CHUNK
    kc = lax.broadcasted_iota(jnp.int32, (tq, nk), 1) // CHUNK
    visible = (kc >= qc) & (kc <= qc + ATT_PAST // CHUNK)
    for hh in range(ATT_GROUP):
        row = jnp.broadcast_to(rel_ref[hh], (tq, REL_RING))
        toeplitz = pltpu.roll(row, 0, 1, stride=1, stride_axis=0)
        bias_ref[hh] = jnp.where(visible, toeplitz[:, :nk] * LOG2E, NEG_INF)


def _head_lanes(hh):
    lane = lax.broadcasted_iota(jnp.int32, (1, ATT_LANES), 1)
    return (lane >= hh * ATT_DH) & (lane < (hh + 1) * ATT_DH)


def _head_values(v):
    return [jnp.where(_head_lanes(hh), v, jnp.ones_like(v)) for hh in range(ATT_GROUP)]


def _attend(q, k, vhs, bias_ref, boff):
    nk = k.shape[0]
    q = q.astype(F32) * (ATT_DH ** -0.5 * LOG2E)
    out = None
    for hh in range(ATT_GROUP):
        mine = _head_lanes(hh)
        qh = jnp.where(mine, q, 0.0).astype(BF16)
        s = lax.dot_general(qh, k, (((1,), (1,)), ((), ())), preferred_element_type=F32)
        s = s + bias_ref[hh, :, boff:boff + nk]
        e = jnp.exp2(s - jnp.max(s, axis=-1, keepdims=True)).astype(BF16)
        pv = jnp.dot(e, vhs[hh], preferred_element_type=F32)
        o = pv / pltpu.roll(pv, ATT_DH, 1)
        out = o if out is None else jnp.where(mine, o, out)
    return out.astype(BF16)


def _band_prompt_kernel(rel_ref, q_ref, k_ref, v_ref, o_ref, bias_ref):
    _, tq, nk = bias_ref.shape
    n_blocks = q_ref.shape[0] // tq
    n_past = ATT_PAST // tq

    @pl.when(pl.program_id(1) == 0)
    def _():
        _build_bias(rel_ref, bias_ref)

    vhs = _head_values(v_ref[...])
    for tt in range(n_blocks):
        rows = slice(tt * tq, (tt + 1) * tq)
        keys = slice(max(tt - n_past, 0) * tq, (tt + 1) * tq)
        o_ref[rows, :] = _attend(q_ref[rows, :], k_ref[keys, :], [vh[keys, :] for vh in vhs], bias_ref,
                                 max(n_past - tt, 0) * tq)


def _band_prompt(p, rel, batch, t):
    w = ATT_LANES
    cq, ck, cv = COL_QB // w, COL_KB // w, COL_VB // w
    return pl.pallas_call(
        _band_prompt_kernel,
        grid=(ATT_HEADS // ATT_GROUP, batch),
        in_specs=[
            pl.BlockSpec((ATT_GROUP, 1, REL_RING), lambda g, b: (g, 0, 0)),
            pl.BlockSpec((t, w), lambda g, b: (b, cq + g)),
            pl.BlockSpec((t, w), lambda g, b: (b, ck + g)),
            pl.BlockSpec((t, w), lambda g, b: (b, cv + g)),
        ],
        out_specs=pl.BlockSpec((t, w), lambda g, b: (b, g)),
        out_shape=jax.ShapeDtypeStruct((batch * t, ATT_WIDTH), BF16),
        scratch_shapes=[pltpu.VMEM((ATT_GROUP, ATT_TQ, ATT_PAST + ATT_TQ), F32)],
        compiler_params=_params("arbitrary", "arbitrary"),
        name="band_prompt",
    )(rel, p, p, p)


def _attend_stacked(q, k, v, bias):
    tq = q.shape[0]
    q = q.astype(F32) * (ATT_DH ** -0.5 * LOG2E)
    qs = jnp.concatenate([jnp.where(_head_lanes(hh), q, 0.0) for hh in range(ATT_GROUP)], axis=0)
    s = lax.dot_general(qs.astype(BF16), k, (((1,), (1,)), ((), ())), preferred_element_type=F32) + bias
    e = jnp.exp2(s - jnp.max(s, axis=-1, keepdims=True))
    pv = jnp.dot(e.astype(BF16), v, preferred_element_type=F32) / jnp.sum(e, axis=-1, keepdims=True)
    out = pv[:tq]
    for hh in range(1, ATT_GROUP):
        out = jnp.where(_head_lanes(hh), pv[hh * tq:(hh + 1) * tq], out)
    return out.astype(BF16)


def _band_step_kernel(rel_ref, q_ref, k_ref, v_ref, ck_ref, cv_ref, o_ref, bias_ref):
    _build_bias(rel_ref, bias_ref)
    batch = ck_ref.shape[0]
    t = q_ref.shape[0] // batch
    bias = bias_ref[...].reshape(ATT_GROUP * t, bias_ref.shape[2])
    for b in range(batch):
        rows = slice(b * t, (b + 1) * t)
        k = jnp.concatenate([ck_ref[b], k_ref[rows, :]], axis=0)
        v = jnp.concatenate([cv_ref[b], v_ref[rows, :]], axis=0)
        o_ref[rows, :] = _attend_stacked(q_ref[rows, :], k, v, bias)


def _band_step(p, cache_k, cache_v, rel, batch, t):
    w = ATT_LANES
    c = cache_k.shape[1]
    cq, ck, cv = COL_QB // w, COL_KB // w, COL_VB // w
    return pl.pallas_call(
        _band_step_kernel,
        grid=(ATT_HEADS // ATT_GROUP,),
        in_specs=[
            pl.BlockSpec((ATT_GROUP, 1, REL_RING), lambda g: (g, 0, 0)),
            pl.BlockSpec((batch * t, w), lambda g: (0, cq + g)),
            pl.BlockSpec((batch * t, w), lambda g: (0, ck + g)),
            pl.BlockSpec((batch * t, w), lambda g: (0, cv + g)),
            pl.BlockSpec((batch, c, w), lambda g: (0, 0, g)),
            pl.BlockSpec((batch, c, w), lambda g: (0, 0, g)),
        ],
        out_specs=pl.BlockSpec((batch * t, w), lambda g: (0, g)),
        out_shape=jax.ShapeDtypeStruct((batch * t, ATT_WIDTH), BF16),
        scratch_shapes=[pltpu.VMEM((ATT_GROUP, t, c + t), F32)],
        compiler_params=_params("parallel"),
        name="band_step",
    )(rel, p, p, p, cache_k, cache_v)


SQUARE_STEPS = 8


def _mix_ffn2_kernel(h_ref, oa_ref, ob_ref, ga_ref, gb_ref, wbg_ref, wba_ref, wo_ref, g2_ref,
                     win_ref, wout_ref, gf_ref, y_ref, wbg_s, wba_s, wo_s, win_s, wout_s):
    step = pl.program_id(0)
    _stage(step, wbg_ref, wbg_s, 0)
    _stage(step, wba_ref, wba_s, 0)
    _stage(step, wo_ref, wo_s, 0)
    _stage(step, win_ref, win_s, 1)
    _stage(step, wout_ref, wout_s, 0)

    @pl.when(step >= W_STEPS)
    def _():
        a = jnp.dot(oa_ref[...], wbg_s[...], preferred_element_type=F32)
        b = jnp.dot(ob_ref[...], wba_s[...], preferred_element_type=F32)
        mixed = (jax.nn.sigmoid(ga_ref[...].astype(F32)) * a
                 + jax.nn.sigmoid(gb_ref[...].astype(F32)) * b)
        h = h_ref[...] + jnp.dot(mixed.astype(BF16), wo_s[...], preferred_element_type=F32)
        xn = _rms(h, g2_ref[...]).astype(BF16)
        h = h + 0.5 * _swiglu(xn, win_s, wout_s)
        y_ref[...] = _rms(h, gf_ref[...])


def _mix_ffn2(h, oa, ob, p, wbg, wba, wo, g2, win, wout, gf, tm):
    m = h.shape[0]
    tile = _row_tile(tm)
    gate = lambda col: pl.BlockSpec((tm, D_MODEL), lambda i: (jnp.maximum(i - W_STEPS, 0), col // D_MODEL))
    square = _slab_spec(wbg.shape, 0, SQUARE_STEPS)
    return pl.pallas_call(
        _mix_ffn2_kernel,
        grid=(W_STEPS + m // tm,),
        in_specs=[tile, tile, tile, gate(COL_GA), gate(COL_GB), square, square, square,
                  _const_spec(g2.shape), _slab_spec(win.shape, 1, W_STEPS),
                  _slab_spec(wout.shape, 0, W_STEPS), _const_spec(gf.shape)],
        out_specs=tile,
        out_shape=jax.ShapeDtypeStruct((m, D_MODEL), F32),
        scratch_shapes=[pltpu.VMEM(wbg.shape, BF16), pltpu.VMEM(wba.shape, BF16), pltpu.VMEM(wo.shape, BF16),
                        pltpu.VMEM(win.shape, BF16), pltpu.VMEM(wout.shape, BF16)],
        compiler_params=_params("arbitrary"),
        name="mix_ffn2",
    )(h, oa, ob, p, p, wbg, wba, wo, g2, win, wout, gf)


def _layer(x, cache_k, cache_v, s0, w, tm):
    batch, t, _ = x.shape
    m = batch * t
    h, n = _ffn1(x.reshape(m, D_MODEL), w["g1"], w["ffn1_in"], w["ffn1_out"], w["gmix"], tm)
    p = _proj(n, w["w_in"], tm)
    o_a, s_new = _gla(p, w["wg"], w["bg"], w["gn"], s0, batch, t)
    if cache_k is None:
        o_b = _band_prompt(p, w["rel"], batch, t)
    else:
        assert cache_k.shape[1] == ATT_PAST and t <= ATT_TQ
        o_b = _band_step(p, cache_k, cache_v, w["rel"], batch, t)
    y = _mix_ffn2(h, o_a, o_b, p, w["wbg"], w["wba"], w["wo"], w["g2"], w["ffn2_in"], w["ffn2_out"],
                  w["gf"], tm)
    keep = min(ATT_PAST, t)
    p3 = p.reshape(batch, t, PROJ_COLS)
    k_keep = p3[:, t - keep:, COL_KB:COL_KB + ATT_WIDTH].astype(F32).reshape(1, batch, keep, ATT_HEADS, ATT_DH)
    v_keep = p3[:, t - keep:, COL_VB:COL_VB + ATT_WIDTH].astype(F32).reshape(1, batch, keep, ATT_HEADS, ATT_DH)
    return y.reshape(batch, t, D_MODEL), k_keep, v_keep, s_new[None]


def kernel(x_prompt, x_sample, cache_att_k, cache_att_v, state_gla, norm_ffn1, w_ffn1_in, w_ffn1_out,
           norm_mix, w_in, w_gla_gate, b_gla_gate, gla_norm, attn_rel_bias, w_branch_gla, w_branch_att,
           w_out, norm_ffn2, w_ffn2_in, w_ffn2_out, norm_final):
    assert norm_ffn1.shape[0] == 1, "single layer"
    wi = w_in[0]
    fa0 = COL_QB
    w_perm = jnp.concatenate(
        [wi[:, :fa0], wi[:, fa0 + GLA_GATE_RANK:], wi[:, fa0:fa0 + GLA_GATE_RANK],
         jnp.zeros((D_MODEL, LANES - GLA_GATE_RANK), wi.dtype)], axis=1).astype(BF16)
    wg = jnp.concatenate([w_gla_gate[0], jnp.zeros((LANES - GLA_GATE_RANK, GLA_K_WIDTH), F32)],
                         axis=0).astype(BF16)
    w = dict(
        g1=norm_ffn1, ffn1_in=w_ffn1_in[0], ffn1_out=w_ffn1_out[0],
        gmix=norm_mix, w_in=w_perm, wg=wg, bg=b_gla_gate, gn=gla_norm, rel=_rel_row(attn_rel_bias[0]),
        wbg=w_branch_gla[0], wba=w_branch_att[0], wo=w_out[0],
        g2=norm_ffn2, ffn2_in=w_ffn2_in[0], ffn2_out=w_ffn2_out[0],
        gf=norm_final.reshape(1, D_MODEL),
    )
    bp = x_prompt.shape[0]
    s0_prompt = jnp.zeros((bp, GLA_HEADS, GLA_DK, GLA_DV), F32)
    yp, kp, vp, sp = _layer(x_prompt, None, None, s0_prompt, w, tm=512)
    bs, c = cache_att_k.shape[1], cache_att_k.shape[2]
    ck = cache_att_k[0].astype(BF16).reshape(bs, c, ATT_WIDTH)
    cv = cache_att_v[0].astype(BF16).reshape(bs, c, ATT_WIDTH)
    ys, ks, vs, ss = _layer(x_sample, ck, cv, state_gla[0], w, tm=256)
    return yp, ys, kp, vp, sp, ks, vs, ss
```

```python
import functools

import jax
import jax.numpy as jnp
import numpy as np
from jax import lax
from jax.experimental import pallas as pl
from jax.experimental.pallas import tpu as pltpu

F32 = jnp.float32
BF16 = jnp.bfloat16

D_MODEL = 1024
CHUNK = 64
GLA_HEADS = 4
GLA_DK = 128
GLA_DV = 256
GLA_K_WIDTH = GLA_HEADS * GLA_DK
GLA_V_WIDTH = GLA_HEADS * GLA_DV
GLA_GATE_RANK = 16
GLA_TAU = 16.0
ATT_HEADS = 16
ATT_DH = 64
ATT_WIDTH = ATT_HEADS * ATT_DH
ATT_PAST = 8 * CHUNK
REL_CLIP = 128
D_FF = 2816
EPS = 1e-6
NEG_INF = -1e30
LOG2E = 1.4426950408889634

LANES = 128
VMEM_LIMIT = 60 * 1024 * 1024

COL_QA = 0
COL_KA = COL_QA + GLA_K_WIDTH
COL_VA = COL_KA + GLA_K_WIDTH
COL_RA = COL_VA + GLA_V_WIDTH
COL_QB = COL_RA + GLA_V_WIDTH
COL_KB = COL_QB + ATT_WIDTH
COL_VB = COL_KB + ATT_WIDTH
COL_GA = COL_VB + ATT_WIDTH
COL_GB = COL_GA + D_MODEL
COL_FA = COL_GB + D_MODEL
PROJ_COLS = COL_FA + LANES


def _const_spec(shape):
    nd = len(shape)
    return pl.BlockSpec(shape, lambda *_: (0,) * nd, pipeline_mode=pl.Buffered(1))


def _params(*sem):
    return pltpu.CompilerParams(dimension_semantics=sem, vmem_limit_bytes=VMEM_LIMIT)


def _rms(x, g):
    return x * lax.rsqrt(jnp.mean(x * x, axis=-1, keepdims=True) + EPS) * g


def _swiglu(xn, win_ref, wout_ref):
    gate = jnp.dot(xn, win_ref[:, :D_FF], preferred_element_type=F32)
    up = jnp.dot(xn, win_ref[:, D_FF:], preferred_element_type=F32)
    act = (gate * jax.nn.sigmoid(gate) * up).astype(BF16)
    return jnp.dot(act, wout_ref[...], preferred_element_type=F32)


W_STEPS = 11


def _slab_spec(shape, axis, steps):
    block = tuple(d // steps if a == axis else d for a, d in enumerate(shape))
    assert block[axis] * steps == shape[axis] and steps <= W_STEPS
    if axis == 0:
        return pl.BlockSpec(block, lambda i: (jnp.minimum(i, steps - 1), 0))
    return pl.BlockSpec(block, lambda i: (0, jnp.minimum(i, steps - 1)))


def _stage(step, src_ref, dst_ref, axis):
    n = src_ref.shape[axis]
    for c in range(dst_ref.shape[axis] // n):
        @pl.when(step == c)
        def _(c=c):
            if axis == 0:
                dst_ref[c * n:(c + 1) * n, :] = src_ref[...].astype(BF16)
            else:
                dst_ref[:, c * n:(c + 1) * n] = src_ref[...].astype(BF16)


ROW_TILE = 512


def _row_tiles(first, count, cols=D_MODEL, col=0, **kw):
    return pl.BlockSpec((ROW_TILE, cols), lambda i: (jnp.clip(i - W_STEPS - first, 0, count - 1), col), **kw)


def _ffn1_kernel(xp_ref, xs_ref, g1_ref, win_ref, wout_ref, gmix_ref, h_ref, n_ref, win_s, wout_s, *,
                 n_prompt):
    step = pl.program_id(0)
    _stage(step, win_ref, win_s, 1)
    _stage(step, wout_ref, wout_s, 0)

    def tile(x_ref):
        x = x_ref[...]
        xn = _rms(x, g1_ref[...]).astype(BF16)
        h = x + 0.5 * _swiglu(xn, win_s, wout_s)
        h_ref[...] = h
        n_ref[...] = _rms(h, gmix_ref[...]).astype(BF16)

    @pl.when((step >= W_STEPS) & (step < W_STEPS + n_prompt))
    def _():
        tile(xp_ref)

    @pl.when(step >= W_STEPS + n_prompt)
    def _():
        tile(xs_ref)


def _ffn1(xp, xs, g1, win, wout, gmix):
    n_prompt, n_sample = xp.shape[0] // ROW_TILE, xs.shape[0] // ROW_TILE
    tiles = n_prompt + n_sample
    m = tiles * ROW_TILE
    return pl.pallas_call(
        functools.partial(_ffn1_kernel, n_prompt=n_prompt),
        grid=(W_STEPS + tiles,),
        in_specs=[_row_tiles(0, n_prompt), _row_tiles(n_prompt, n_sample, pipeline_mode=pl.Buffered(1)),
                  _const_spec(g1.shape), _slab_spec(win.shape, 1, W_STEPS),
                  _slab_spec(wout.shape, 0, W_STEPS), _const_spec(gmix.shape)],
        out_specs=[_row_tiles(0, tiles), _row_tiles(0, tiles)],
        out_shape=[jax.ShapeDtypeStruct((m, D_MODEL), F32), jax.ShapeDtypeStruct((m, D_MODEL), BF16)],
        scratch_shapes=[pltpu.VMEM(win.shape, BF16), pltpu.VMEM(wout.shape, BF16)],
        compiler_params=_params("arbitrary"),
        name="ffn1",
    )(xp, xs, g1, win, wout, gmix)


PROJ_STEP = 1024


def _proj_kernel(n_ref, w_ref, o_ref, kp_ref, vp_ref, ks_ref, vs_ref, *, n_prompt, seq_tiles):
    step = pl.program_id(0)
    n = n_ref[...]
    kept = {}
    for c0 in range(0, PROJ_COLS, PROJ_STEP):
        c1 = min(c0 + PROJ_STEP, PROJ_COLS)
        acc = jnp.dot(n, w_ref[:, c0:c1], preferred_element_type=F32)
        o_ref[:, c0:c1] = acc.astype(BF16)
        if c0 in (COL_KB, COL_VB):
            kept[c0] = acc

    @pl.when((step < n_prompt) & (step % seq_tiles == seq_tiles - 1))
    def _():
        kp_ref[...] = kept[COL_KB].reshape(kp_ref.shape)
        vp_ref[...] = kept[COL_VB].reshape(vp_ref.shape)

    @pl.when(step >= n_prompt)
    def _():
        ks_ref[...] = kept[COL_KB].reshape(ks_ref.shape)
        vs_ref[...] = kept[COL_VB].reshape(vs_ref.shape)


def _proj(n, w, bp, tp, bs, ts):
    m = n.shape[0]
    n_prompt, seq_tiles = bp * tp // ROW_TILE, tp // ROW_TILE
    assert PROJ_STEP == ATT_WIDTH and min(ATT_PAST, tp) == ROW_TILE and bs * ts == ROW_TILE and ts <= ATT_PAST
    row = lambda i: (i, 0)
    once = dict(pipeline_mode=pl.Buffered(1))
    kept_p = pl.BlockSpec((None, None, ROW_TILE, ATT_HEADS, ATT_DH),
                          lambda i: (0, jnp.minimum(i // seq_tiles, bp - 1), 0, 0, 0), **once)
    kept_s = pl.BlockSpec((None, bs, ts, ATT_HEADS, ATT_DH), lambda i: (0, 0, 0, 0, 0), **once)
    return pl.pallas_call(
        functools.partial(_proj_kernel, n_prompt=n_prompt, seq_tiles=seq_tiles),
        grid=(m // ROW_TILE,),
        in_specs=[pl.BlockSpec((ROW_TILE, D_MODEL), row), _const_spec(w.shape)],
        out_specs=[pl.BlockSpec((ROW_TILE, PROJ_COLS), row), kept_p, kept_p, kept_s, kept_s],
        out_shape=[jax.ShapeDtypeStruct((m, PROJ_COLS), BF16)]
        + [jax.ShapeDtypeStruct((1, bp, ROW_TILE, ATT_HEADS, ATT_DH), F32)] * 2
        + [jax.ShapeDtypeStruct((1, bs, ts, ATT_HEADS, ATT_DH), F32)] * 2,
        compiler_params=_params("arbitrary"),
        name="proj",
    )(n, w)


GLA_ROWS = 2048


def _gla_kernel(q_ref, k_ref, v_ref, r_ref, f_ref, wg_ref, bg_ref, gn_ref, s0_ref, *rest, n_seq):
    o_ref, s_ref = rest[-2:]
    cl = CHUNK
    nc = q_ref.shape[0] // cl
    per_seq = nc // n_seq
    bmm = functools.partial(jnp.einsum, preferred_element_type=F32)

    z = jnp.dot(f_ref[...], wg_ref[...], preferred_element_type=F32) + bg_ref[...]
    log_a = (jnp.minimum(z, 0.0) - jnp.log(1.0 + jnp.exp(-jnp.abs(z)))) * (LOG2E / GLA_TAU)
    log_a = log_a.reshape(nc, cl, GLA_DK)

    ri = lax.broadcasted_iota(jnp.int32, (cl, cl), 0)
    ci = lax.broadcasted_iota(jnp.int32, (cl, cl), 1)
    causal = ri >= ci
    tri = jnp.broadcast_to(causal.astype(BF16), (nc, cl, cl))
    hi = log_a.astype(BF16)
    lo = (log_a - hi.astype(F32)).astype(BF16)
    b = bmm('nij,njd->nid', tri, hi) + bmm('nij,njd->nid', tri, lo)
    b_last = b[:, cl - 1:cl, :]

    q = q_ref[...].astype(F32).reshape(nc, cl, GLA_DK) * (GLA_DK ** -0.5)
    k = k_ref[...].astype(F32).reshape(nc, cl, GLA_DK)
    v = v_ref[...].reshape(nc, cl, GLA_DV)
    q_dec = (q * jnp.exp2(b)).astype(BF16)
    k_inv = (k * jnp.exp2(-b)).astype(BF16)
    k_end = (k * jnp.exp2(b_last - b)).astype(BF16)

    scores = jnp.where(causal, bmm('nid,njd->nij', q_dec, k_inv), 0.0)
    o_intra = bmm('nij,njv->niv', scores.astype(BF16), v)
    incr_t = bmm('njv,njd->nvd', v, k_end)
    decay = jnp.exp2(b_last)

    before = []
    for i in range(n_seq):
        s = s0_ref[i].T
        for c in range(i * per_seq, (i + 1) * per_seq):
            before.append(s.astype(BF16))
            s = s * decay[c] + incr_t[c]
        s_ref[i] = s.T
    s_before = jnp.stack(before)
    o = o_intra + bmm('nid,nvd->niv', q_dec, s_before)

    o = o * lax.rsqrt(jnp.mean(o * o, axis=-1, keepdims=True) + EPS) * gn_ref[...]
    r = r_ref[...].astype(F32)
    o = o.reshape(nc * cl, GLA_DV) * (r * jax.nn.sigmoid(r))
    o_ref[...] = o.astype(BF16)


def _gla(p, wg, bg, gn, s0, batch, t, row0, o_prev=None):
    hq, hk = COL_QA // GLA_DK, COL_KA // GLA_DK
    hv, hr = COL_VA // GLA_DV, COL_RA // GLA_DV
    n_seq = min(batch, max(1, GLA_ROWS // t))
    rows = n_seq * t
    assert row0 % rows == 0 and batch % n_seq == 0
    b0 = row0 // rows
    state = pl.BlockSpec((n_seq, None, GLA_DK, GLA_DV), lambda b, h: (b, h, 0, 0))
    extra = [] if o_prev is None else [o_prev]
    n_in = 9
    return pl.pallas_call(
        functools.partial(_gla_kernel, n_seq=n_seq),
        grid=(batch // n_seq, GLA_HEADS),
        in_specs=[
            pl.BlockSpec((rows, GLA_DK), lambda b, h: (b0 + b, hq + h)),
            pl.BlockSpec((rows, GLA_DK), lambda b, h: (b0 + b, hk + h)),
            pl.BlockSpec((rows, GLA_DV), lambda b, h: (b0 + b, hv + h)),
            pl.BlockSpec((rows, GLA_DV), lambda b, h: (b0 + b, hr + h)),
            pl.BlockSpec((rows, LANES), lambda b, h: (b0 + b, COL_FA // LANES)),
            pl.BlockSpec((LANES, GLA_DK), lambda b, h: (0, h)),
            pl.BlockSpec((1, GLA_DK), lambda b, h: (0, h)),
            pl.BlockSpec((1, GLA_DV), lambda b, h: (0, h)),
            state,
        ] + [pl.BlockSpec(memory_space=pl.ANY)] * len(extra),
        out_specs=[pl.BlockSpec((rows, GLA_DV), lambda b, h: (b0 + b, h)), state],
        out_shape=[jax.ShapeDtypeStruct((p.shape[0], GLA_V_WIDTH), BF16),
                   jax.ShapeDtypeStruct((batch, GLA_HEADS, GLA_DK, GLA_DV), F32)],
        input_output_aliases={n_in: 0} if extra else {},
        compiler_params=_params("parallel", "parallel"),
        name="gla",
    )(p, p, p, p, p, wg, bg, gn, s0, *extra)


ATT_GROUP = 4
ATT_LANES = ATT_GROUP * ATT_DH
ATT_TQ = 256
REL_RING = 1024


def _rel_row(table):
    u = np.arange(REL_RING)
    u = np.where(u < ATT_PAST + ATT_TQ, u, u - REL_RING)
    idx = np.clip(ATT_PAST - u, -REL_CLIP, REL_CLIP) + REL_CLIP
    return table[:, idx].astype(F32)[:, None, :]


def _build_bias(rel_ref, bias_ref):
    _, tq, nk = bias_ref.shape
    qc = lax.broadcasted_iota(jnp.int32, (tq, nk), 0) // CHUNK
    kc = lax.broadcasted_iota(jnp.int32, (tq, nk), 1) // CHUNK
    visible = (kc >= qc) & (kc <= qc + ATT_PAST // CHUNK)
    for hh in range(ATT_GROUP):
        row = jnp.broadcast_to(rel_ref[hh], (tq, REL_RING))
        toeplitz = pltpu.roll(row, 0, 1, stride=1, stride_axis=0)
        bias_ref[hh] = jnp.where(visible, toeplitz[:, :nk] * LOG2E, NEG_INF)


def _head_lanes(hh):
    lane = lax.broadcasted_iota(jnp.int32, (1, ATT_LANES), 1)
    return (lane >= hh * ATT_DH) & (lane < (hh + 1) * ATT_DH)


def _head_values(v):
    return [jnp.where(_head_lanes(hh), v, jnp.ones_like(v)) for hh in range(ATT_GROUP)]


def _attend(q, k, vhs, bias_ref, boff):
    nk = k.shape[0]
    q = q.astype(F32) * (ATT_DH ** -0.5 * LOG2E)
    out = None
    for hh in range(ATT_GROUP):
        mine = _head_lanes(hh)
        qh = jnp.where(mine, q, 0.0).astype(BF16)
        s = lax.dot_general(qh, k, (((1,), (1,)), ((), ())), preferred_element_type=F32)
        s = s + bias_ref[hh, :, boff:boff + nk]
        e = jnp.exp2(s - jnp.max(s, axis=-1, keepdims=True)).astype(BF16)
        pv = jnp.dot(e, vhs[hh], preferred_element_type=F32)
        o = pv / pltpu.roll(pv, ATT_DH, 1)
        out = o if out is None else jnp.where(mine, o, out)
    return out.astype(BF16)


def _band_prompt_kernel(rel_ref, q_ref, k_ref, v_ref, o_ref, bias_ref):
    _, tq, nk = bias_ref.shape
    n_blocks = q_ref.shape[0] // tq
    n_past = ATT_PAST // tq

    @pl.when(pl.program_id(1) == 0)
    def _():
        _build_bias(rel_ref, bias_ref)

    vhs = _head_values(v_ref[...])
    for tt in range(n_blocks):
        rows = slice(tt * tq, (tt + 1) * tq)
        keys = slice(max(tt - n_past, 0) * tq, (tt + 1) * tq)
        o_ref[rows, :] = _attend(q_ref[rows, :], k_ref[keys, :], [vh[keys, :] for vh in vhs], bias_ref,
                                 max(n_past - tt, 0) * tq)


def _band_prompt(p, rel, batch, t):
    w = ATT_LANES
    cq, ck, cv = COL_QB // w, COL_KB // w, COL_VB // w
    return pl.pallas_call(
        _band_prompt_kernel,
        grid=(ATT_HEADS // ATT_GROUP, batch),
        in_specs=[
            pl.BlockSpec((ATT_GROUP, 1, REL_RING), lambda g, b: (g, 0, 0)),
            pl.BlockSpec((t, w), lambda g, b: (b, cq + g)),
            pl.BlockSpec((t, w), lambda g, b: (b, ck + g)),
            pl.BlockSpec((t, w), lambda g, b: (b, cv + g)),
        ],
        out_specs=pl.BlockSpec((t, w), lambda g, b: (b, g)),
        out_shape=jax.ShapeDtypeStruct((p.shape[0], ATT_WIDTH), BF16),
        scratch_shapes=[pltpu.VMEM((ATT_GROUP, ATT_TQ, ATT_PAST + ATT_TQ), F32)],
        compiler_params=_params("arbitrary", "arbitrary"),
        name="band_prompt",
    )(rel, p, p, p)


def _attend_stacked(q, k, v, bias):
    tq = q.shape[0]
    q = q.astype(F32) * (ATT_DH ** -0.5 * LOG2E)
    qs = jnp.concatenate([jnp.where(_head_lanes(hh), q, 0.0) for hh in range(ATT_GROUP)], axis=0)
    s = lax.dot_general(qs.astype(BF16), k, (((1,), (1,)), ((), ())), preferred_element_type=F32) + bias
    e = jnp.exp2(s - jnp.max(s, axis=-1, keepdims=True))
    pv = jnp.dot(e.astype(BF16), v, preferred_element_type=F32) / jnp.sum(e, axis=-1, keepdims=True)
    out = pv[:tq]
    for hh in range(1, ATT_GROUP):
        out = jnp.where(_head_lanes(hh), pv[hh * tq:(hh + 1) * tq], out)
    return out.astype(BF16)


def _band_step_kernel(rel_ref, q_ref, k_ref, v_ref, ck_ref, cv_ref, prev_ref, o_ref, bias_ref):
    del prev_ref
    _build_bias(rel_ref, bias_ref)
    batch = ck_ref.shape[0]
    t = q_ref.shape[0] // batch
    bias = bias_ref[...].reshape(ATT_GROUP * t, bias_ref.shape[2])
    for b in range(batch):
        rows = slice(b * t, (b + 1) * t)
        k = jnp.concatenate([ck_ref[b], k_ref[rows, :]], axis=0)
        v = jnp.concatenate([cv_ref[b], v_ref[rows, :]], axis=0)
        o_ref[rows, :] = _attend_stacked(q_ref[rows, :], k, v, bias)


def _band_step(p, cache_k, cache_v, rel, batch, t, row0, o_prev):
    w = ATT_LANES
    c = cache_k.shape[1]
    rows = batch * t
    assert row0 % rows == 0
    r0 = row0 // rows
    cq, ck, cv = COL_QB // w, COL_KB // w, COL_VB // w
    return pl.pallas_call(
        _band_step_kernel,
        grid=(ATT_HEADS // ATT_GROUP,),
        in_specs=[
            pl.BlockSpec((ATT_GROUP, 1, REL_RING), lambda g: (g, 0, 0)),
            pl.BlockSpec((rows, w), lambda g: (r0, cq + g)),
            pl.BlockSpec((rows, w), lambda g: (r0, ck + g)),
            pl.BlockSpec((rows, w), lambda g: (r0, cv + g)),
            pl.BlockSpec((batch, c, w), lambda g: (0, 0, g)),
            pl.BlockSpec((batch, c, w), lambda g: (0, 0, g)),
            pl.BlockSpec(memory_space=pl.ANY),
        ],
        out_specs=pl.BlockSpec((rows, w), lambda g: (r0, g)),
        out_shape=jax.ShapeDtypeStruct(o_prev.shape, BF16),
        input_output_aliases={6: 0},
        scratch_shapes=[pltpu.VMEM((ATT_GROUP, t, c + t), F32)],
        compiler_params=_params("parallel"),
        name="band_step",
    )(rel, p, p, p, cache_k, cache_v, o_prev)


SQUARE_STEPS = 8


def _mix_ffn2_kernel(h_ref, oa_ref, ob_ref, ga_ref, gb_ref, wbg_ref, wba_ref, wo_ref, g2_ref,
                     win_ref, wout_ref, gf_ref, yp_ref, ys_ref, wbg_s, wba_s, wo_s, win_s, wout_s, *,
                     n_prompt):
    step = pl.program_id(0)
    _stage(step, wbg_ref, wbg_s, 0)
    _stage(step, wba_ref, wba_s, 0)
    _stage(step, wo_ref, wo_s, 0)
    _stage(step, win_ref, win_s, 1)
    _stage(step, wout_ref, wout_s, 0)

    @pl.when(step >= W_STEPS)
    def _():
        a = jnp.dot(oa_ref[...], wbg_s[...], preferred_element_type=F32)
        b = jnp.dot(ob_ref[...], wba_s[...], preferred_element_type=F32)
        mixed = (jax.nn.sigmoid(ga_ref[...].astype(F32)) * a
                 + jax.nn.sigmoid(gb_ref[...].astype(F32)) * b)
        h = h_ref[...] + jnp.dot(mixed.astype(BF16), wo_s[...], preferred_element_type=F32)
        xn = _rms(h, g2_ref[...]).astype(BF16)
        h = h + 0.5 * _swiglu(xn, win_s, wout_s)
        y = _rms(h, gf_ref[...])

        @pl.when(step < W_STEPS + n_prompt)
        def _():
            yp_ref[...] = y

        @pl.when(step >= W_STEPS + n_prompt)
        def _():
            ys_ref[...] = y


def _mix_ffn2(h, oa, ob, p, wbg, wba, wo, g2, win, wout, gf, n_prompt):
    tiles = h.shape[0] // ROW_TILE
    n_sample = tiles - n_prompt
    tile = _row_tiles(0, tiles)
    square = _slab_spec(wbg.shape, 0, SQUARE_STEPS)
    return pl.pallas_call(
        functools.partial(_mix_ffn2_kernel, n_prompt=n_prompt),
        grid=(W_STEPS + tiles,),
        in_specs=[tile, tile, tile, _row_tiles(0, tiles, col=COL_GA // D_MODEL),
                  _row_tiles(0, tiles, col=COL_GB // D_MODEL), square, square, square,
                  _const_spec(g2.shape), _slab_spec(win.shape, 1, W_STEPS),
                  _slab_spec(wout.shape, 0, W_STEPS), _const_spec(gf.shape)],
        out_specs=[_row_tiles(0, n_prompt), _row_tiles(n_prompt, n_sample, pipeline_mode=pl.Buffered(1))],
        out_shape=[jax.ShapeDtypeStruct((n_prompt * ROW_TILE, D_MODEL), F32),
                   jax.ShapeDtypeStruct((n_sample * ROW_TILE, D_MODEL), F32)],
        scratch_shapes=[pltpu.VMEM(wbg.shape, BF16), pltpu.VMEM(wba.shape, BF16), pltpu.VMEM(wo.shape, BF16),
                        pltpu.VMEM(win.shape, BF16), pltpu.VMEM(wout.shape, BF16)],
        compiler_params=_params("arbitrary"),
        name="mix_ffn2",
    )(h, oa, ob, p, p, wbg, wba, wo, g2, win, wout, gf)


def _kept_rows(p, row0, batch, t, col):
    keep = min(ATT_PAST, t)
    rows = p[row0:row0 + batch * t, col:col + ATT_WIDTH].reshape(batch, t, ATT_WIDTH)[:, t - keep:]
    return rows.astype(F32).reshape(1, batch, keep, ATT_HEADS, ATT_DH)


def kernel(x_prompt, x_sample, cache_att_k, cache_att_v, state_gla, norm_ffn1, w_ffn1_in, w_ffn1_out,
           norm_mix, w_in, w_gla_gate, b_gla_gate, gla_norm, attn_rel_bias, w_branch_gla, w_branch_att,
           w_out, norm_ffn2, w_ffn2_in, w_ffn2_out, norm_final):
    assert norm_ffn1.shape[0] == 1, "single layer"
    bp, tp, _ = x_prompt.shape
    bs, ts, _ = x_sample.shape
    mp, ms = bp * tp, bs * ts
    assert mp % ROW_TILE == 0 and ms % ROW_TILE == 0
    assert cache_att_k.shape[2] == ATT_PAST and ts <= ATT_TQ

    wi = w_in[0]
    fa0 = COL_QB
    w_perm = jnp.concatenate(
        [wi[:, :fa0], wi[:, fa0 + GLA_GATE_RANK:], wi[:, fa0:fa0 + GLA_GATE_RANK],
         jnp.zeros((D_MODEL, LANES - GLA_GATE_RANK), wi.dtype)], axis=1).astype(BF16)
    wg = jnp.concatenate([w_gla_gate[0], jnp.zeros((LANES - GLA_GATE_RANK, GLA_K_WIDTH), F32)],
                         axis=0).astype(BF16)
    rel = _rel_row(attn_rel_bias[0])
    ck = cache_att_k[0].astype(BF16).reshape(bs, ATT_PAST, ATT_WIDTH)
    cv = cache_att_v[0].astype(BF16).reshape(bs, ATT_PAST, ATT_WIDTH)

    h, n = _ffn1(x_prompt.reshape(mp, D_MODEL), x_sample.reshape(ms, D_MODEL), norm_ffn1, w_ffn1_in[0],
                 w_ffn1_out[0], norm_mix)
    p, k_prompt, v_prompt, k_sample, v_sample = _proj(n, w_perm, bp, tp, bs, ts)
    gla_w = (wg, b_gla_gate, gla_norm)
    o_a, s_prompt = _gla(p, *gla_w, jnp.zeros((bp, GLA_HEADS, GLA_DK, GLA_DV), F32), bp, tp, 0)
    o_a, s_sample = _gla(p, *gla_w, state_gla[0], bs, ts, mp, o_a)
    o_b = _band_prompt(p, rel, bp, tp)
    o_b = _band_step(p, ck, cv, rel, bs, ts, mp, o_b)
    yp, ys = _mix_ffn2(h, o_a, o_b, p, w_branch_gla[0], w_branch_att[0], w_out[0], norm_ffn2, w_ffn2_in[0],
                       w_ffn2_out[0], norm_final.reshape(1, D_MODEL), mp // ROW_TILE)
    return (yp.reshape(bp, tp, D_MODEL), ys.reshape(bs, ts, D_MODEL), k_prompt, v_prompt, s_prompt[None],
            k_sample, v_sample, s_sample[None])
```

```python
import functools

import jax
import jax.numpy as jnp
import numpy as np
from jax import lax
from jax.experimental import pallas as pl
from jax.experimental.pallas import tpu as pltpu

F32 = jnp.float32
BF16 = jnp.bfloat16

D_MODEL = 1024
CHUNK = 64
GLA_HEADS = 4
GLA_DK = 128
GLA_DV = 256
GLA_K_WIDTH = GLA_HEADS * GLA_DK
GLA_V_WIDTH = GLA_HEADS * GLA_DV
GLA_GATE_RANK = 16
GLA_TAU = 16.0
ATT_HEADS = 16
ATT_DH = 64
ATT_WIDTH = ATT_HEADS * ATT_DH
ATT_PAST = 8 * CHUNK
REL_CLIP = 128
D_FF = 2816
EPS = 1e-6
NEG_INF = -1e30
LOG2E = 1.4426950408889634

LANES = 128
VMEM_LIMIT = 60 * 1024 * 1024

COL_QA = 0
COL_KA = COL_QA + GLA_K_WIDTH
COL_VA = COL_KA + GLA_K_WIDTH
COL_RA = COL_VA + GLA_V_WIDTH
COL_QB = COL_RA + GLA_V_WIDTH
COL_KB = COL_QB + ATT_WIDTH
COL_VB = COL_KB + ATT_WIDTH
COL_GA = COL_VB + ATT_WIDTH
COL_GB = COL_GA + D_MODEL
COL_FA = COL_GB + D_MODEL
PROJ_COLS = COL_FA + LANES


def _const_spec(shape):
    nd = len(shape)
    return pl.BlockSpec(shape, lambda *_: (0,) * nd, pipeline_mode=pl.Buffered(1))


def _params(*sem):
    return pltpu.CompilerParams(dimension_semantics=sem, vmem_limit_bytes=VMEM_LIMIT)


def _rms(x, g):
    return x * lax.rsqrt(jnp.mean(x * x, axis=-1, keepdims=True) + EPS) * g


def _swiglu(xn, win_ref, wout_ref):
    gate = jnp.dot(xn, win_ref[:, :D_FF], preferred_element_type=F32)
    up = jnp.dot(xn, win_ref[:, D_FF:], preferred_element_type=F32)
    act = (gate * jax.nn.sigmoid(gate) * up).astype(BF16)
    return jnp.dot(act, wout_ref[...], preferred_element_type=F32)


W_STEPS = 16
FFN_SLABS = 11


def _slab_spec(shape, axis, steps, **kw):
    block = tuple(d // steps if a == axis else d for a, d in enumerate(shape))
    assert block[axis] * steps == shape[axis] and steps <= W_STEPS
    if axis == 0:
        return pl.BlockSpec(block, lambda i: (jnp.minimum(i, steps - 1), 0), **kw)
    return pl.BlockSpec(block, lambda i: (0, jnp.minimum(i, steps - 1)), **kw)


def _stage(step, src_ref, dst_ref, axis):
    n = src_ref.shape[axis]
    for c in range(dst_ref.shape[axis] // n):
        @pl.when(step == c)
        def _(c=c):
            if axis == 0:
                dst_ref[c * n:(c + 1) * n, :] = src_ref[...].astype(BF16)
            else:
                dst_ref[:, c * n:(c + 1) * n] = src_ref[...].astype(BF16)


ROW_TILE = 512


def _row_tiles(first, count, cols=D_MODEL, col=0, **kw):
    return pl.BlockSpec((ROW_TILE, cols), lambda i: (jnp.clip(i - W_STEPS - first, 0, count - 1), col), **kw)


def _ffn1_kernel(xp_ref, xs_ref, g1_ref, win_ref, wout_ref, gmix_ref, h_ref, n_ref, win_s, wout_s, *,
                 n_prompt):
    step = pl.program_id(0)
    _stage(step, win_ref, win_s, 1)
    _stage(step, wout_ref, wout_s, 0)

    def tile(x_ref):
        x = x_ref[...]
        xn = _rms(x, g1_ref[...]).astype(BF16)
        h = x + 0.5 * _swiglu(xn, win_s, wout_s)
        h_ref[...] = h
        n_ref[...] = _rms(h, gmix_ref[...]).astype(BF16)

    @pl.when((step >= W_STEPS) & (step < W_STEPS + n_prompt))
    def _():
        tile(xp_ref)

    @pl.when(step >= W_STEPS + n_prompt)
    def _():
        tile(xs_ref)


def _ffn1(xp, xs, g1, win, wout, gmix):
    n_prompt, n_sample = xp.shape[0] // ROW_TILE, xs.shape[0] // ROW_TILE
    tiles = n_prompt + n_sample
    m = tiles * ROW_TILE
    return pl.pallas_call(
        functools.partial(_ffn1_kernel, n_prompt=n_prompt),
        grid=(W_STEPS + tiles,),
        in_specs=[_row_tiles(0, n_prompt), _row_tiles(n_prompt, n_sample, pipeline_mode=pl.Buffered(1)),
                  _const_spec(g1.shape), _slab_spec(win.shape, 1, FFN_SLABS),
                  _slab_spec(wout.shape, 0, FFN_SLABS), _const_spec(gmix.shape)],
        out_specs=[_row_tiles(0, tiles), _row_tiles(0, tiles)],
        out_shape=[jax.ShapeDtypeStruct((m, D_MODEL), F32), jax.ShapeDtypeStruct((m, D_MODEL), BF16)],
        scratch_shapes=[pltpu.VMEM(win.shape, BF16), pltpu.VMEM(wout.shape, BF16)],
        compiler_params=_params("arbitrary"),
        name="ffn1",
    )(xp, xs, g1, win, wout, gmix)


PROJ_STEP = 1024


W_IN_SLABS = 16
W_IN_GATE0 = COL_QB


def _stage_w_in(step, src_ref, dst_ref):
    n = src_ref.shape[0]
    g0, g1 = W_IN_GATE0, W_IN_GATE0 + GLA_GATE_RANK
    for c in range(dst_ref.shape[0] // n):
        @pl.when(step == c)
        def _(c=c):
            rows = slice(c * n, (c + 1) * n)
            dst_ref[rows, :g0] = src_ref[:, :g0].astype(BF16)
            dst_ref[rows, g0:COL_FA] = src_ref[:, g1:].astype(BF16)
            dst_ref[rows, COL_FA:COL_FA + GLA_GATE_RANK] = src_ref[:, g0:g1].astype(BF16)
            dst_ref[rows, COL_FA + GLA_GATE_RANK:] = jnp.zeros((n, LANES - GLA_GATE_RANK), BF16)


def _proj_kernel(n_ref, w_ref, o_ref, kp_ref, vp_ref, w_s, *, n_prompt, seq_tiles):
    step = pl.program_id(0)
    _stage_w_in(step, w_ref, w_s)

    @pl.when(step >= W_STEPS)
    def _():
        tile = step - W_STEPS
        n = n_ref[...]
        kept = {}
        for c0 in range(0, PROJ_COLS, PROJ_STEP):
            c1 = min(c0 + PROJ_STEP, PROJ_COLS)
            acc = jnp.dot(n, w_s[:, c0:c1], preferred_element_type=F32)
            o_ref[:, c0:c1] = acc.astype(BF16)
            if c0 in (COL_KB, COL_VB):
                kept[c0] = acc

        @pl.when((tile < n_prompt) & (tile % seq_tiles == seq_tiles - 1))
        def _():
            kp_ref[...] = kept[COL_KB].reshape(kp_ref.shape)
            vp_ref[...] = kept[COL_VB].reshape(vp_ref.shape)


def _proj(n, w, bp, tp):
    tiles = n.shape[0] // ROW_TILE
    n_prompt, seq_tiles = bp * tp // ROW_TILE, tp // ROW_TILE
    assert PROJ_STEP == ATT_WIDTH and min(ATT_PAST, tp) == ROW_TILE
    kept = pl.BlockSpec((None, None, ROW_TILE, ATT_HEADS, ATT_DH),
                        lambda i: (0, jnp.clip((i - W_STEPS) // seq_tiles, 0, bp - 1), 0, 0, 0))
    return pl.pallas_call(
        functools.partial(_proj_kernel, n_prompt=n_prompt, seq_tiles=seq_tiles),
        grid=(W_STEPS + tiles,),
        in_specs=[_row_tiles(0, tiles), _slab_spec(w.shape, 0, W_IN_SLABS, pipeline_mode=pl.Buffered(1))],
        out_specs=[_row_tiles(0, tiles, cols=PROJ_COLS), kept, kept],
        out_shape=[jax.ShapeDtypeStruct((tiles * ROW_TILE, PROJ_COLS), BF16)]
        + [jax.ShapeDtypeStruct((1, bp, ROW_TILE, ATT_HEADS, ATT_DH), F32)] * 2,
        scratch_shapes=[pltpu.VMEM((D_MODEL, PROJ_COLS), BF16)],
        compiler_params=_params("arbitrary"),
        name="proj",
    )(n, w)


GLA_ROWS = 2048


def _gla_kernel(q_ref, k_ref, v_ref, r_ref, f_ref, wg_ref, bg_ref, gn_ref, s0_ref, *rest, n_seq):
    o_ref, s_ref = rest[-2:]
    cl = CHUNK
    nc = q_ref.shape[0] // cl
    per_seq = nc // n_seq
    bmm = functools.partial(jnp.einsum, preferred_element_type=F32)

    z = jnp.dot(f_ref[...], wg_ref[...], preferred_element_type=F32) + bg_ref[...]
    log_a = (jnp.minimum(z, 0.0) - jnp.log(1.0 + jnp.exp(-jnp.abs(z)))) * (LOG2E / GLA_TAU)
    log_a = log_a.reshape(nc, cl, GLA_DK)

    ri = lax.broadcasted_iota(jnp.int32, (cl, cl), 0)
    ci = lax.broadcasted_iota(jnp.int32, (cl, cl), 1)
    causal = ri >= ci
    tri = jnp.broadcast_to(causal.astype(BF16), (nc, cl, cl))
    hi = log_a.astype(BF16)
    lo = (log_a - hi.astype(F32)).astype(BF16)
    b = bmm('nij,njd->nid', tri, hi) + bmm('nij,njd->nid', tri, lo)
    b_last = b[:, cl - 1:cl, :]

    q = q_ref[...].astype(F32).reshape(nc, cl, GLA_DK) * (GLA_DK ** -0.5)
    k = k_ref[...].astype(F32).reshape(nc, cl, GLA_DK)
    v = v_ref[...].reshape(nc, cl, GLA_DV)
    q_dec = (q * jnp.exp2(b)).astype(BF16)
    k_inv = (k * jnp.exp2(-b)).astype(BF16)
    k_end = (k * jnp.exp2(b_last - b)).astype(BF16)

    scores = jnp.where(causal, bmm('nid,njd->nij', q_dec, k_inv), 0.0)
    o_intra = bmm('nij,njv->niv', scores.astype(BF16), v)
    incr_t = bmm('njv,njd->nvd', v, k_end)
    decay = jnp.exp2(b_last)

    before = []
    for i in range(n_seq):
        s = s0_ref[i].T
        for c in range(i * per_seq, (i + 1) * per_seq):
            before.append(s.astype(BF16))
            s = s * decay[c] + incr_t[c]
        s_ref[i] = s.T
    s_before = jnp.stack(before)
    o = o_intra + bmm('nid,nvd->niv', q_dec, s_before)

    o = o * lax.rsqrt(jnp.mean(o * o, axis=-1, keepdims=True) + EPS) * gn_ref[...]
    r = r_ref[...].astype(F32)
    o = o.reshape(nc * cl, GLA_DV) * (r * jax.nn.sigmoid(r))
    o_ref[...] = o.astype(BF16)


def _gla(p, wg, bg, gn, s0, batch, t, row0, o_prev=None):
    hq, hk = COL_QA // GLA_DK, COL_KA // GLA_DK
    hv, hr = COL_VA // GLA_DV, COL_RA // GLA_DV
    n_seq = min(batch, max(1, GLA_ROWS // t))
    rows = n_seq * t
    assert row0 % rows == 0 and batch % n_seq == 0
    b0 = row0 // rows
    state = pl.BlockSpec((n_seq, None, GLA_DK, GLA_DV), lambda b, h: (b, h, 0, 0))
    extra = [] if o_prev is None else [o_prev]
    n_in = 9
    return pl.pallas_call(
        functools.partial(_gla_kernel, n_seq=n_seq),
        grid=(batch // n_seq, GLA_HEADS),
        in_specs=[
            pl.BlockSpec((rows, GLA_DK), lambda b, h: (b0 + b, hq + h)),
            pl.BlockSpec((rows, GLA_DK), lambda b, h: (b0 + b, hk + h)),
            pl.BlockSpec((rows, GLA_DV), lambda b, h: (b0 + b, hv + h)),
            pl.BlockSpec((rows, GLA_DV), lambda b, h: (b0 + b, hr + h)),
            pl.BlockSpec((rows, LANES), lambda b, h: (b0 + b, COL_FA // LANES)),
            pl.BlockSpec((LANES, GLA_DK), lambda b, h: (0, h)),
            pl.BlockSpec((1, GLA_DK), lambda b, h: (0, h)),
            pl.BlockSpec((1, GLA_DV), lambda b, h: (0, h)),
            state,
        ] + [pl.BlockSpec(memory_space=pl.ANY)] * len(extra),
        out_specs=[pl.BlockSpec((rows, GLA_DV), lambda b, h: (b0 + b, h)), state],
        out_shape=[jax.ShapeDtypeStruct((p.shape[0], GLA_V_WIDTH), BF16),
                   jax.ShapeDtypeStruct((batch, GLA_HEADS, GLA_DK, GLA_DV), F32)],
        input_output_aliases={n_in: 0} if extra else {},
        compiler_params=_params("parallel", "parallel"),
        name="gla",
    )(p, p, p, p, p, wg, bg, gn, s0, *extra)


ATT_GROUP = 4
ATT_LANES = ATT_GROUP * ATT_DH
ATT_TQ = 256
REL_RING = 1024


def _rel_row(table):
    u = np.arange(REL_RING)
    u = np.where(u < ATT_PAST + ATT_TQ, u, u - REL_RING)
    idx = np.clip(ATT_PAST - u, -REL_CLIP, REL_CLIP) + REL_CLIP
    return table[:, idx].astype(F32)[:, None, :]


def _build_bias(rel_ref, bias_ref):
    _, tq, nk = bias_ref.shape
    qc = lax.broadcasted_iota(jnp.int32, (tq, nk), 0) // CHUNK
    kc = lax.broadcasted_iota(jnp.int32, (tq, nk), 1) // CHUNK
    visible = (kc >= qc) & (kc <= qc + ATT_PAST // CHUNK)
    for hh in range(bias_ref.shape[0]):
        row = jnp.broadcast_to(rel_ref[hh], (tq, REL_RING))
        toeplitz = pltpu.roll(row, 0, 1, stride=1, stride_axis=0)
        bias_ref[hh] = jnp.where(visible, toeplitz[:, :nk] * LOG2E, NEG_INF)


def _head_lanes(hh):
    lane = lax.broadcasted_iota(jnp.int32, (1, ATT_LANES), 1)
    return (lane >= hh * ATT_DH) & (lane < (hh + 1) * ATT_DH)


def _head_values(v):
    return [jnp.where(_head_lanes(hh), v, jnp.ones_like(v)) for hh in range(ATT_GROUP)]


def _attend(q, k, vhs, bias_ref, boff):
    nk = k.shape[0]
    q = q.astype(F32) * (ATT_DH ** -0.5 * LOG2E)
    out = None
    for hh in range(ATT_GROUP):
        mine = _head_lanes(hh)
        qh = jnp.where(mine, q, 0.0).astype(BF16)
        s = lax.dot_general(qh, k, (((1,), (1,)), ((), ())), preferred_element_type=F32)
        s = s + bias_ref[hh, :, boff:boff + nk]
        e = jnp.exp2(s - jnp.max(s, axis=-1, keepdims=True)).astype(BF16)
        pv = jnp.dot(e, vhs[hh], preferred_element_type=F32)
        o = pv / pltpu.roll(pv, ATT_DH, 1)
        out = o if out is None else jnp.where(mine, o, out)
    return out.astype(BF16)


def _band_prompt_kernel(rel_ref, q_ref, k_ref, v_ref, o_ref, bias_ref):
    _, tq, nk = bias_ref.shape
    n_blocks = q_ref.shape[0] // tq
    n_past = ATT_PAST // tq

    @pl.when(pl.program_id(1) == 0)
    def _():
        _build_bias(rel_ref, bias_ref)

    vhs = _head_values(v_ref[...])
    for tt in range(n_blocks):
        rows = slice(tt * tq, (tt + 1) * tq)
        keys = slice(max(tt - n_past, 0) * tq, (tt + 1) * tq)
        o_ref[rows, :] = _attend(q_ref[rows, :], k_ref[keys, :], [vh[keys, :] for vh in vhs], bias_ref,
                                 max(n_past - tt, 0) * tq)


def _band_prompt(p, rel, batch, t):
    w = ATT_LANES
    cq, ck, cv = COL_QB // w, COL_KB // w, COL_VB // w
    return pl.pallas_call(
        _band_prompt_kernel,
        grid=(ATT_HEADS // ATT_GROUP, batch),
        in_specs=[
            pl.BlockSpec((ATT_GROUP, 1, REL_RING), lambda g, b: (g, 0, 0)),
            pl.BlockSpec((t, w), lambda g, b: (b, cq + g)),
            pl.BlockSpec((t, w), lambda g, b: (b, ck + g)),
            pl.BlockSpec((t, w), lambda g, b: (b, cv + g)),
        ],
        out_specs=pl.BlockSpec((t, w), lambda g, b: (b, g)),
        out_shape=jax.ShapeDtypeStruct((p.shape[0], ATT_WIDTH), BF16),
        scratch_shapes=[pltpu.VMEM((ATT_GROUP, ATT_TQ, ATT_PAST + ATT_TQ), F32)],
        compiler_params=_params("arbitrary", "arbitrary"),
        name="band_prompt",
    )(rel, p, p, p)


def _attend_stacked(q, k, v, bias):
    tq = q.shape[0]
    q = q.astype(F32) * (ATT_DH ** -0.5 * LOG2E)
    qs = jnp.concatenate([jnp.where(_head_lanes(hh), q, 0.0) for hh in range(ATT_GROUP)], axis=0)
    s = lax.dot_general(qs.astype(BF16), k, (((1,), (1,)), ((), ())), preferred_element_type=F32) + bias
    e = jnp.exp2(s - jnp.max(s, axis=-1, keepdims=True))
    pv = jnp.dot(e.astype(BF16), v, preferred_element_type=F32) / jnp.sum(e, axis=-1, keepdims=True)
    out = pv[:tq]
    for hh in range(1, ATT_GROUP):
        out = jnp.where(_head_lanes(hh), pv[hh * tq:(hh + 1) * tq], out)
    return out.astype(BF16)


def _band_step_kernel(rel_ref, q_ref, k_ref, v_ref, ck_ref, cv_ref, prev_ref, o_ref, bias_ref):
    del prev_ref

    @pl.when(pl.program_id(0) == 0)
    def _():
        _build_bias(rel_ref, bias_ref)

    t, nk = bias_ref.shape[1:]
    c = ck_ref.shape[0]
    k = jnp.concatenate([ck_ref[...].reshape(c, ATT_WIDTH).astype(BF16), k_ref[...]], axis=0)
    v = jnp.concatenate([cv_ref[...].reshape(c, ATT_WIDTH).astype(BF16), v_ref[...]], axis=0)
    for g in range(ATT_HEADS // ATT_GROUP):
        lanes = slice(g * ATT_LANES, (g + 1) * ATT_LANES)
        bias = bias_ref[g * ATT_GROUP:(g + 1) * ATT_GROUP].reshape(ATT_GROUP * t, nk)
        o_ref[:, lanes] = _attend_stacked(q_ref[:, lanes], k[:, lanes], v[:, lanes], bias)


def _band_step(p, cache_k, cache_v, rel, batch, t, row0, o_prev):
    c = cache_k.shape[2]
    assert row0 % t == 0
    r0 = row0 // t
    new = lambda col: pl.BlockSpec((t, ATT_WIDTH), lambda b: (r0 + b, col // ATT_WIDTH))
    cache = pl.BlockSpec((None, None, c, ATT_HEADS, ATT_DH), lambda b: (0, b, 0, 0, 0))
    return pl.pallas_call(
        _band_step_kernel,
        grid=(batch,),
        in_specs=[_const_spec(rel.shape), new(COL_QB), new(COL_KB), new(COL_VB), cache, cache,
                  pl.BlockSpec(memory_space=pl.ANY)],
        out_specs=pl.BlockSpec((t, ATT_WIDTH), lambda b: (r0 + b, 0)),
        out_shape=jax.ShapeDtypeStruct(o_prev.shape, BF16),
        input_output_aliases={6: 0},
        scratch_shapes=[pltpu.VMEM((ATT_HEADS, t, c + t), F32)],
        compiler_params=_params("arbitrary"),
        name="band_step",
    )(rel, p, p, p, cache_k, cache_v, o_prev)


SQUARE_STEPS = 8


def _mix_ffn2_kernel(h_ref, oa_ref, ob_ref, ga_ref, gb_ref, wbg_ref, wba_ref, wo_ref, g2_ref,
                     win_ref, wout_ref, gf_ref, yp_ref, ys_ref, wbg_s, wba_s, wo_s, win_s, wout_s, *,
                     n_prompt):
    step = pl.program_id(0)
    _stage(step, wbg_ref, wbg_s, 0)
    _stage(step, wba_ref, wba_s, 0)
    _stage(step, wo_ref, wo_s, 0)
    _stage(step, win_ref, win_s, 1)
    _stage(step, wout_ref, wout_s, 0)

    @pl.when(step >= W_STEPS)
    def _():
        a = jnp.dot(oa_ref[...], wbg_s[...], preferred_element_type=F32)
        b = jnp.dot(ob_ref[...], wba_s[...], preferred_element_type=F32)
        mixed = (jax.nn.sigmoid(ga_ref[...].astype(F32)) * a
                 + jax.nn.sigmoid(gb_ref[...].astype(F32)) * b)
        h = h_ref[...] + jnp.dot(mixed.astype(BF16), wo_s[...], preferred_element_type=F32)
        xn = _rms(h, g2_ref[...]).astype(BF16)
        h = h + 0.5 * _swiglu(xn, win_s, wout_s)
        y = _rms(h, gf_ref[...])

        @pl.when(step < W_STEPS + n_prompt)
        def _():
            yp_ref[...] = y

        @pl.when(step >= W_STEPS + n_prompt)
        def _():
            ys_ref[...] = y


def _mix_ffn2(h, oa, ob, p, wbg, wba, wo, g2, win, wout, gf, n_prompt):
    tiles = h.shape[0] // ROW_TILE
    n_sample = tiles - n_prompt
    tile = _row_tiles(0, tiles)
    square = _slab_spec(wbg.shape, 0, SQUARE_STEPS)
    return pl.pallas_call(
        functools.partial(_mix_ffn2_kernel, n_prompt=n_prompt),
        grid=(W_STEPS + tiles,),
        in_specs=[tile, tile, tile, _row_tiles(0, tiles, col=COL_GA // D_MODEL),
                  _row_tiles(0, tiles, col=COL_GB // D_MODEL), square, square, square,
                  _const_spec(g2.shape), _slab_spec(win.shape, 1, FFN_SLABS),
                  _slab_spec(wout.shape, 0, FFN_SLABS), _const_spec(gf.shape)],
        out_specs=[_row_tiles(0, n_prompt), _row_tiles(n_prompt, n_sample, pipeline_mode=pl.Buffered(1))],
        out_shape=[jax.ShapeDtypeStruct((n_prompt * ROW_TILE, D_MODEL), F32),
                   jax.ShapeDtypeStruct((n_sample * ROW_TILE, D_MODEL), F32)],
        scratch_shapes=[pltpu.VMEM(wbg.shape, BF16), pltpu.VMEM(wba.shape, BF16), pltpu.VMEM(wo.shape, BF16),
                        pltpu.VMEM(win.shape, BF16), pltpu.VMEM(wout.shape, BF16)],
        compiler_params=_params("arbitrary"),
        name="mix_ffn2",
    )(h, oa, ob, p, p, wbg, wba, wo, g2, win, wout, gf)


def _kept_rows(p, row0, batch, t, col):
    keep = min(ATT_PAST, t)
    rows = p[row0:row0 + batch * t, col:col + ATT_WIDTH].reshape(batch, t, ATT_WIDTH)[:, t - keep:]
    return rows.astype(F32).reshape(1, batch, keep, ATT_HEADS, ATT_DH)


def kernel(x_prompt, x_sample, cache_att_k, cache_att_v, state_gla, norm_ffn1, w_ffn1_in, w_ffn1_out,
           norm_mix, w_in, w_gla_gate, b_gla_gate, gla_norm, attn_rel_bias, w_branch_gla, w_branch_att,
           w_out, norm_ffn2, w_ffn2_in, w_ffn2_out, norm_final):
    assert norm_ffn1.shape[0] == 1, "single layer"
    bp, tp, _ = x_prompt.shape
    bs, ts, _ = x_sample.shape
    mp, ms = bp * tp, bs * ts
    assert mp % ROW_TILE == 0 and ms % ROW_TILE == 0
    assert cache_att_k.shape[2] == ATT_PAST and ts <= ATT_TQ

    wg = jnp.concatenate([w_gla_gate[0], jnp.zeros((LANES - GLA_GATE_RANK, GLA_K_WIDTH), F32)],
                         axis=0).astype(BF16)
    rel = _rel_row(attn_rel_bias[0])

    h, n = _ffn1(x_prompt.reshape(mp, D_MODEL), x_sample.reshape(ms, D_MODEL), norm_ffn1, w_ffn1_in[0],
                 w_ffn1_out[0], norm_mix)
    p, k_prompt, v_prompt = _proj(n, w_in[0], bp, tp)
    gla_w = (wg, b_gla_gate, gla_norm)
    o_a, s_prompt = _gla(p, *gla_w, jnp.zeros((bp, GLA_HEADS, GLA_DK, GLA_DV), F32), bp, tp, 0)
    o_a, s_sample = _gla(p, *gla_w, state_gla[0], bs, ts, mp, o_a)
    o_b = _band_prompt(p, rel, bp, tp)
    o_b = _band_step(p, cache_att_k, cache_att_v, rel, bs, ts, mp, o_b)
    yp, ys = _mix_ffn2(h, o_a, o_b, p, w_branch_gla[0], w_branch_att[0], w_out[0], norm_ffn2, w_ffn2_in[0],
                       w_ffn2_out[0], norm_final.reshape(1, D_MODEL), mp // ROW_TILE)
    return (yp.reshape(bp, tp, D_MODEL), ys.reshape(bs, ts, D_MODEL), k_prompt, v_prompt, s_prompt[None],
            _kept_rows(p, mp, bs, ts, COL_KB), _kept_rows(p, mp, bs, ts, COL_VB), s_sample[None])
```

```python
import functools

import jax
import jax.numpy as jnp
import numpy as np
from jax import lax
from jax.experimental import pallas as pl
from jax.experimental.pallas import tpu as pltpu

F32 = jnp.float32
BF16 = jnp.bfloat16

D_MODEL = 1024
CHUNK = 64
GLA_HEADS = 4
GLA_DK = 128
GLA_DV = 256
GLA_K_WIDTH = GLA_HEADS * GLA_DK
GLA_V_WIDTH = GLA_HEADS * GLA_DV
GLA_GATE_RANK = 16
GLA_TAU = 16.0
ATT_HEADS = 16
ATT_DH = 64
ATT_WIDTH = ATT_HEADS * ATT_DH
ATT_PAST = 8 * CHUNK
REL_CLIP = 128
D_FF = 2816
EPS = 1e-6
NEG_INF = -1e30
LOG2E = 1.4426950408889634

LANES = 128
VMEM_LIMIT = 60 * 1024 * 1024

COL_QA = 0
COL_KA = COL_QA + GLA_K_WIDTH
COL_VA = COL_KA + GLA_K_WIDTH
COL_RA = COL_VA + GLA_V_WIDTH
COL_QB = COL_RA + GLA_V_WIDTH
COL_KB = COL_QB + ATT_WIDTH
COL_VB = COL_KB + ATT_WIDTH
COL_GA = COL_VB + ATT_WIDTH
COL_GB = COL_GA + D_MODEL
COL_FA = COL_GB + D_MODEL
PROJ_COLS = COL_FA + LANES


def _const_spec(shape):
    nd = len(shape)
    return pl.BlockSpec(shape, lambda *_: (0,) * nd, pipeline_mode=pl.Buffered(1))


def _params(*sem):
    return pltpu.CompilerParams(dimension_semantics=sem, vmem_limit_bytes=VMEM_LIMIT)


def _rms(x, g):
    return x * lax.rsqrt(jnp.mean(x * x, axis=-1, keepdims=True) + EPS) * g


def _swiglu(xn, win_ref, wout_ref):
    gate = jnp.dot(xn, win_ref[:, :D_FF], preferred_element_type=F32)
    up = jnp.dot(xn, win_ref[:, D_FF:], preferred_element_type=F32)
    act = (gate * jax.nn.sigmoid(gate) * up).astype(BF16)
    return jnp.dot(act, wout_ref[...], preferred_element_type=F32)


W_STEPS = 16
FFN_SLABS = 11


def _slab_spec(shape, axis, steps, **kw):
    block = tuple(d // steps if a == axis else d for a, d in enumerate(shape))
    assert block[axis] * steps == shape[axis] and steps <= W_STEPS
    if axis == 0:
        return pl.BlockSpec(block, lambda i: (jnp.minimum(i, steps - 1), 0), **kw)
    return pl.BlockSpec(block, lambda i: (0, jnp.minimum(i, steps - 1)), **kw)


def _stage(step, src_ref, dst_ref, axis):
    n = src_ref.shape[axis]
    for c in range(dst_ref.shape[axis] // n):
        @pl.when(step == c)
        def _(c=c):
            if axis == 0:
                dst_ref[c * n:(c + 1) * n, :] = src_ref[...].astype(BF16)
            else:
                dst_ref[:, c * n:(c + 1) * n] = src_ref[...].astype(BF16)


ROW_TILE = 512


def _row_tiles(first, count, cols=D_MODEL, col=0, **kw):
    return pl.BlockSpec((ROW_TILE, cols), lambda i: (jnp.clip(i - W_STEPS - first, 0, count - 1), col), **kw)


def _ffn1_kernel(xp_ref, xs_ref, g1_ref, win_ref, wout_ref, gmix_ref, h_ref, n_ref, win_s, wout_s, *,
                 n_prompt):
    step = pl.program_id(0)
    _stage(step, win_ref, win_s, 1)
    _stage(step, wout_ref, wout_s, 0)

    def tile(x_ref):
        x = x_ref[...]
        xn = _rms(x, g1_ref[...]).astype(BF16)
        h = x + 0.5 * _swiglu(xn, win_s, wout_s)
        h_ref[...] = h
        n_ref[...] = _rms(h, gmix_ref[...]).astype(BF16)

    @pl.when((step >= W_STEPS) & (step < W_STEPS + n_prompt))
    def _():
        tile(xp_ref)

    @pl.when(step >= W_STEPS + n_prompt)
    def _():
        tile(xs_ref)


def _ffn1(xp, xs, g1, win, wout, gmix):
    n_prompt, n_sample = xp.shape[0] // ROW_TILE, xs.shape[0] // ROW_TILE
    tiles = n_prompt + n_sample
    m = tiles * ROW_TILE
    return pl.pallas_call(
        functools.partial(_ffn1_kernel, n_prompt=n_prompt),
        grid=(W_STEPS + tiles,),
        in_specs=[_row_tiles(0, n_prompt), _row_tiles(n_prompt, n_sample, pipeline_mode=pl.Buffered(1)),
                  _const_spec(g1.shape), _slab_spec(win.shape, 1, FFN_SLABS),
                  _slab_spec(wout.shape, 0, FFN_SLABS), _const_spec(gmix.shape)],
        out_specs=[_row_tiles(0, tiles), _row_tiles(0, tiles)],
        out_shape=[jax.ShapeDtypeStruct((m, D_MODEL), F32), jax.ShapeDtypeStruct((m, D_MODEL), BF16)],
        scratch_shapes=[pltpu.VMEM(win.shape, BF16), pltpu.VMEM(wout.shape, BF16)],
        compiler_params=_params("arbitrary"),
        name="ffn1",
    )(xp, xs, g1, win, wout, gmix)


PROJ_STEP = 1024


W_IN_SLAB = 1024
W_IN_GATE0 = COL_QB


def _stage_w_in(step, src_ref, dst_ref, n_cols):
    n = src_ref.shape[0]
    g0, g1 = W_IN_GATE0, W_IN_GATE0 + GLA_GATE_RANK
    moves = [(0, g0, 0), (g0, g1, COL_FA), (g1, n_cols, g0)]
    for c in range(pl.cdiv(n_cols, n)):
        @pl.when(step == c)
        def _(c=c):
            lo, hi = c * n, min((c + 1) * n, n_cols)
            for first, end, new in moves:
                a, b = max(lo, first), min(hi, end)
                if a < b:
                    dst_ref[new + a - first:new + b - first, :] = src_ref[a - lo:b - lo, :].astype(BF16)
            if c == 0:
                pad = dst_ref.shape[0] - COL_FA - GLA_GATE_RANK
                dst_ref[COL_FA + GLA_GATE_RANK:, :] = jnp.zeros((pad, dst_ref.shape[1]), BF16)


def _proj_kernel(n_ref, w_ref, o_ref, kp_ref, vp_ref, w_s, *, n_prompt, seq_tiles, n_cols):
    step = pl.program_id(0)
    _stage_w_in(step, w_ref, w_s, n_cols)

    @pl.when(step >= W_STEPS)
    def _():
        tile = step - W_STEPS
        n = n_ref[...]
        kept = {}
        for c0 in range(0, PROJ_COLS, PROJ_STEP):
            c1 = min(c0 + PROJ_STEP, PROJ_COLS)
            acc = lax.dot_general(n, w_s[c0:c1, :], (((1,), (1,)), ((), ())), preferred_element_type=F32)
            o_ref[:, c0:c1] = acc.astype(BF16)
            if c0 in (COL_KB, COL_VB):
                kept[c0] = acc

        @pl.when((tile < n_prompt) & (tile % seq_tiles == seq_tiles - 1))
        def _():
            kp_ref[...] = kept[COL_KB].T.reshape(kp_ref.shape)
            vp_ref[...] = kept[COL_VB].T.reshape(vp_ref.shape)


def _proj(n, w_t, bp, tp):
    tiles = n.shape[0] // ROW_TILE
    n_prompt, seq_tiles = bp * tp // ROW_TILE, tp // ROW_TILE
    n_cols = w_t.shape[0]
    assert PROJ_STEP == ATT_WIDTH and min(ATT_PAST, tp) == ROW_TILE and pl.cdiv(n_cols, W_IN_SLAB) <= W_STEPS
    last_slab = pl.cdiv(n_cols, W_IN_SLAB) - 1
    kept = pl.BlockSpec((None, None, ATT_HEADS, ATT_DH, ROW_TILE),
                        lambda i: (0, jnp.clip((i - W_STEPS) // seq_tiles, 0, bp - 1), 0, 0, 0))
    return pl.pallas_call(
        functools.partial(_proj_kernel, n_prompt=n_prompt, seq_tiles=seq_tiles, n_cols=n_cols),
        grid=(W_STEPS + tiles,),
        in_specs=[_row_tiles(0, tiles),
                  pl.BlockSpec((W_IN_SLAB, D_MODEL), lambda i: (jnp.minimum(i, last_slab), 0))],
        out_specs=[_row_tiles(0, tiles, cols=PROJ_COLS), kept, kept],
        out_shape=[jax.ShapeDtypeStruct((tiles * ROW_TILE, PROJ_COLS), BF16)]
        + [jax.ShapeDtypeStruct((1, bp, ATT_HEADS, ATT_DH, ROW_TILE), F32)] * 2,
        scratch_shapes=[pltpu.VMEM((PROJ_COLS, D_MODEL), BF16)],
        compiler_params=_params("arbitrary"),
        name="proj",
    )(n, w_t)


GLA_ROWS = 2048


def _gla_kernel(q_ref, k_ref, v_ref, r_ref, f_ref, wg_ref, bg_ref, gn_ref, s0_ref, *rest, n_seq):
    o_ref, s_ref = rest[-2:]
    cl = CHUNK
    nc = q_ref.shape[0] // cl
    per_seq = nc // n_seq
    bmm = functools.partial(jnp.einsum, preferred_element_type=F32)

    z = jnp.dot(f_ref[...], wg_ref[...], preferred_element_type=F32) + bg_ref[...]
    log_a = (jnp.minimum(z, 0.0) - jnp.log(1.0 + jnp.exp(-jnp.abs(z)))) * (LOG2E / GLA_TAU)
    log_a = log_a.reshape(nc, cl, GLA_DK)

    ri = lax.broadcasted_iota(jnp.int32, (cl, cl), 0)
    ci = lax.broadcasted_iota(jnp.int32, (cl, cl), 1)
    causal = ri >= ci
    tri = jnp.broadcast_to(causal.astype(BF16), (nc, cl, cl))
    hi = log_a.astype(BF16)
    lo = (log_a - hi.astype(F32)).astype(BF16)
    b = bmm('nij,njd->nid', tri, hi) + bmm('nij,njd->nid', tri, lo)
    b_last = b[:, cl - 1:cl, :]

    q = q_ref[...].astype(F32).reshape(nc, cl, GLA_DK) * (GLA_DK ** -0.5)
    k = k_ref[...].astype(F32).reshape(nc, cl, GLA_DK)
    v = v_ref[...].reshape(nc, cl, GLA_DV)
    q_dec = (q * jnp.exp2(b)).astype(BF16)
    k_inv = (k * jnp.exp2(-b)).astype(BF16)
    k_end = (k * jnp.exp2(b_last - b)).astype(BF16)

    scores = jnp.where(causal, bmm('nid,njd->nij', q_dec, k_inv), 0.0)
    o_intra = bmm('nij,njv->niv', scores.astype(BF16), v)
    incr_t = bmm('njv,njd->nvd', v, k_end)
    decay = jnp.exp2(b_last)

    before = []
    for i in range(n_seq):
        s = s0_ref[i].T
        for c in range(i * per_seq, (i + 1) * per_seq):
            before.append(s.astype(BF16))
            s = s * decay[c] + incr_t[c]
        s_ref[i] = s.T
    s_before = jnp.stack(before)
    o = o_intra + bmm('nid,nvd->niv', q_dec, s_before)

    o = o * lax.rsqrt(jnp.mean(o * o, axis=-1, keepdims=True) + EPS) * gn_ref[...]
    r = r_ref[...].astype(F32)
    o = o.reshape(nc * cl, GLA_DV) * (r * jax.nn.sigmoid(r))
    o_ref[...] = o.astype(BF16)


def _gla(p, wg, bg, gn, s0, batch, t, row0, o_prev=None):
    hq, hk = COL_QA // GLA_DK, COL_KA // GLA_DK
    hv, hr = COL_VA // GLA_DV, COL_RA // GLA_DV
    n_seq = min(batch, max(1, GLA_ROWS // t))
    rows = n_seq * t
    assert row0 % rows == 0 and batch % n_seq == 0
    b0 = row0 // rows
    state = pl.BlockSpec((n_seq, None, GLA_DK, GLA_DV), lambda b, h: (b, h, 0, 0))
    extra = [] if o_prev is None else [o_prev]
    n_in = 9
    return pl.pallas_call(
        functools.partial(_gla_kernel, n_seq=n_seq),
        grid=(batch // n_seq, GLA_HEADS),
        in_specs=[
            pl.BlockSpec((rows, GLA_DK), lambda b, h: (b0 + b, hq + h)),
            pl.BlockSpec((rows, GLA_DK), lambda b, h: (b0 + b, hk + h)),
            pl.BlockSpec((rows, GLA_DV), lambda b, h: (b0 + b, hv + h)),
            pl.BlockSpec((rows, GLA_DV), lambda b, h: (b0 + b, hr + h)),
            pl.BlockSpec((rows, LANES), lambda b, h: (b0 + b, COL_FA // LANES)),
            pl.BlockSpec((LANES, GLA_DK), lambda b, h: (0, h)),
            pl.BlockSpec((1, GLA_DK), lambda b, h: (0, h)),
            pl.BlockSpec((1, GLA_DV), lambda b, h: (0, h)),
            state,
        ] + [pl.BlockSpec(memory_space=pl.ANY)] * len(extra),
        out_specs=[pl.BlockSpec((rows, GLA_DV), lambda b, h: (b0 + b, h)), state],
        out_shape=[jax.ShapeDtypeStruct((p.shape[0], GLA_V_WIDTH), BF16),
                   jax.ShapeDtypeStruct((batch, GLA_HEADS, GLA_DK, GLA_DV), F32)],
        input_output_aliases={n_in: 0} if extra else {},
        compiler_params=_params("parallel", "parallel"),
        name="gla",
    )(p, p, p, p, p, wg, bg, gn, s0, *extra)


ATT_GROUP = 4
ATT_LANES = ATT_GROUP * ATT_DH
ATT_TQ = 256
REL_RING = 1024


def _rel_row(table):
    u = np.arange(REL_RING)
    u = np.where(u < ATT_PAST + ATT_TQ, u, u - REL_RING)
    idx = np.clip(ATT_PAST - u, -REL_CLIP, REL_CLIP) + REL_CLIP
    return table[:, idx].astype(F32)[:, None, :]


def _build_bias(rel_ref, bias_ref):
    _, tq, nk = bias_ref.shape
    qc = lax.broadcasted_iota(jnp.int32, (tq, nk), 0) // CHUNK
    kc = lax.broadcasted_iota(jnp.int32, (tq, nk), 1) // CHUNK
    visible = (kc >= qc) & (kc <= qc + ATT_PAST // CHUNK)
    for hh in range(bias_ref.shape[0]):
        row = jnp.broadcast_to(rel_ref[hh], (tq, REL_RING))
        toeplitz = pltpu.roll(row, 0, 1, stride=1, stride_axis=0)
        bias_ref[hh] = jnp.where(visible, toeplitz[:, :nk] * LOG2E, NEG_INF)


def _head_lanes(hh):
    lane = lax.broadcasted_iota(jnp.int32, (1, ATT_LANES), 1)
    return (lane >= hh * ATT_DH) & (lane < (hh + 1) * ATT_DH)


def _head_values(v):
    return [jnp.where(_head_lanes(hh), v, jnp.ones_like(v)) for hh in range(ATT_GROUP)]


def _attend(q, k, vhs, bias_ref, boff):
    nk = k.shape[0]
    q = q.astype(F32) * (ATT_DH ** -0.5 * LOG2E)
    out = None
    for hh in range(ATT_GROUP):
        mine = _head_lanes(hh)
        qh = jnp.where(mine, q, 0.0).astype(BF16)
        s = lax.dot_general(qh, k, (((1,), (1,)), ((), ())), preferred_element_type=F32)
        s = s + bias_ref[hh, :, boff:boff + nk]
        e = jnp.exp2(s - jnp.max(s, axis=-1, keepdims=True)).astype(BF16)
        pv = jnp.dot(e, vhs[hh], preferred_element_type=F32)
        o = pv / pltpu.roll(pv, ATT_DH, 1)
        out = o if out is None else jnp.where(mine, o, out)
    return out.astype(BF16)


def _band_prompt_kernel(rel_ref, q_ref, k_ref, v_ref, o_ref, bias_ref):
    _, tq, nk = bias_ref.shape
    n_blocks = q_ref.shape[0] // tq
    n_past = ATT_PAST // tq

    @pl.when(pl.program_id(1) == 0)
    def _():
        _build_bias(rel_ref, bias_ref)

    vhs = _head_values(v_ref[...])
    for tt in range(n_blocks):
        rows = slice(tt * tq, (tt + 1) * tq)
        keys = slice(max(tt - n_past, 0) * tq, (tt + 1) * tq)
        o_ref[rows, :] = _attend(q_ref[rows, :], k_ref[keys, :], [vh[keys, :] for vh in vhs], bias_ref,
                                 max(n_past - tt, 0) * tq)


def _band_prompt(p, rel, batch, t):
    w = ATT_LANES
    cq, ck, cv = COL_QB // w, COL_KB // w, COL_VB // w
    return pl.pallas_call(
        _band_prompt_kernel,
        grid=(ATT_HEADS // ATT_GROUP, batch),
        in_specs=[
            pl.BlockSpec((ATT_GROUP, 1, REL_RING), lambda g, b: (g, 0, 0)),
            pl.BlockSpec((t, w), lambda g, b: (b, cq + g)),
            pl.BlockSpec((t, w), lambda g, b: (b, ck + g)),
            pl.BlockSpec((t, w), lambda g, b: (b, cv + g)),
        ],
        out_specs=pl.BlockSpec((t, w), lambda g, b: (b, g)),
        out_shape=jax.ShapeDtypeStruct((p.shape[0], ATT_WIDTH), BF16),
        scratch_shapes=[pltpu.VMEM((ATT_GROUP, ATT_TQ, ATT_PAST + ATT_TQ), F32)],
        compiler_params=_params("arbitrary", "arbitrary"),
        name="band_prompt",
    )(rel, p, p, p)


def _attend_cached(q, kc_t, vc_t, k_new, v_new, bias):
    tq, c = q.shape[0], kc_t.shape[1]
    nt = (((1,), (1,)), ((), ()))
    q = q.astype(F32) * (ATT_DH ** -0.5 * LOG2E)
    qs = jnp.concatenate([jnp.where(_head_lanes(hh), q, 0.0) for hh in range(ATT_GROUP)], axis=0).astype(BF16)
    s = jnp.concatenate([jnp.dot(qs, kc_t, preferred_element_type=F32),
                         lax.dot_general(qs, k_new, nt, preferred_element_type=F32)], axis=1) + bias
    e = jnp.exp2(s - jnp.max(s, axis=-1, keepdims=True))
    eb = e.astype(BF16)
    pv = (lax.dot_general(eb[:, :c], vc_t, nt, preferred_element_type=F32)
          + jnp.dot(eb[:, c:], v_new, preferred_element_type=F32)) / jnp.sum(e, axis=-1, keepdims=True)
    out = pv[:tq]
    for hh in range(1, ATT_GROUP):
        out = jnp.where(_head_lanes(hh), pv[hh * tq:(hh + 1) * tq], out)
    return out.astype(BF16)


def _band_step_kernel(rel_ref, q_ref, k_ref, v_ref, ck_ref, cv_ref, prev_ref, o_ref, bias_ref):
    del prev_ref

    @pl.when(pl.program_id(0) == 0)
    def _():
        _build_bias(rel_ref, bias_ref)

    t, nk = bias_ref.shape[1:]
    c = ck_ref.shape[2]
    kc_t = ck_ref[...].reshape(ATT_WIDTH, c).astype(BF16)
    vc_t = cv_ref[...].reshape(ATT_WIDTH, c).astype(BF16)
    for g in range(ATT_HEADS // ATT_GROUP):
        lanes = slice(g * ATT_LANES, (g + 1) * ATT_LANES)
        bias = bias_ref[g * ATT_GROUP:(g + 1) * ATT_GROUP].reshape(ATT_GROUP * t, nk)
        o_ref[:, lanes] = _attend_cached(q_ref[:, lanes], kc_t[lanes, :], vc_t[lanes, :], k_ref[:, lanes],
                                         v_ref[:, lanes], bias)


def _band_step(p, cache_k_t, cache_v_t, rel, batch, t, row0, o_prev):
    c = cache_k_t.shape[4]
    assert row0 % t == 0
    r0 = row0 // t
    new = lambda col: pl.BlockSpec((t, ATT_WIDTH), lambda b: (r0 + b, col // ATT_WIDTH))
    cache = pl.BlockSpec((None, None, ATT_HEADS, ATT_DH, c), lambda b: (0, b, 0, 0, 0))
    return pl.pallas_call(
        _band_step_kernel,
        grid=(batch,),
        in_specs=[_const_spec(rel.shape), new(COL_QB), new(COL_KB), new(COL_VB), cache, cache,
                  pl.BlockSpec(memory_space=pl.ANY)],
        out_specs=pl.BlockSpec((t, ATT_WIDTH), lambda b: (r0 + b, 0)),
        out_shape=jax.ShapeDtypeStruct(o_prev.shape, BF16),
        input_output_aliases={6: 0},
        scratch_shapes=[pltpu.VMEM((ATT_HEADS, t, c + t), F32)],
        compiler_params=_params("arbitrary"),
        name="band_step",
    )(rel, p, p, p, cache_k_t, cache_v_t, o_prev)


SQUARE_STEPS = 8


def _mix_ffn2_kernel(h_ref, oa_ref, ob_ref, ga_ref, gb_ref, wbg_ref, wba_ref, wo_ref, g2_ref,
                     win_ref, wout_ref, gf_ref, yp_ref, ys_ref, wbg_s, wba_s, wo_s, win_s, wout_s, *,
                     n_prompt):
    step = pl.program_id(0)
    _stage(step, wbg_ref, wbg_s, 0)
    _stage(step, wba_ref, wba_s, 0)
    _stage(step, wo_ref, wo_s, 0)
    _stage(step, win_ref, win_s, 1)
    _stage(step, wout_ref, wout_s, 0)

    @pl.when(step >= W_STEPS)
    def _():
        a = jnp.dot(oa_ref[...], wbg_s[...], preferred_element_type=F32)
        b = jnp.dot(ob_ref[...], wba_s[...], preferred_element_type=F32)
        mixed = (jax.nn.sigmoid(ga_ref[...].astype(F32)) * a
                 + jax.nn.sigmoid(gb_ref[...].astype(F32)) * b)
        h = h_ref[...] + jnp.dot(mixed.astype(BF16), wo_s[...], preferred_element_type=F32)
        xn = _rms(h, g2_ref[...]).astype(BF16)
        h = h + 0.5 * _swiglu(xn, win_s, wout_s)
        y = _rms(h, gf_ref[...])

        @pl.when(step < W_STEPS + n_prompt)
        def _():
            yp_ref[...] = y

        @pl.when(step >= W_STEPS + n_prompt)
        def _():
            ys_ref[...] = y


def _mix_ffn2(h, oa, ob, p, wbg, wba, wo, g2, win, wout, gf, n_prompt):
    tiles = h.shape[0] // ROW_TILE
    n_sample = tiles - n_prompt
    tile = _row_tiles(0, tiles)
    square = _slab_spec(wbg.shape, 0, SQUARE_STEPS)
    return pl.pallas_call(
        functools.partial(_mix_ffn2_kernel, n_prompt=n_prompt),
        grid=(W_STEPS + tiles,),
        in_specs=[tile, tile, tile, _row_tiles(0, tiles, col=COL_GA // D_MODEL),
                  _row_tiles(0, tiles, col=COL_GB // D_MODEL), square, square, square,
                  _const_spec(g2.shape), _slab_spec(win.shape, 1, FFN_SLABS),
                  _slab_spec(wout.shape, 0, FFN_SLABS), _const_spec(gf.shape)],
        out_specs=[_row_tiles(0, n_prompt), _row_tiles(n_prompt, n_sample, pipeline_mode=pl.Buffered(1))],
        out_shape=[jax.ShapeDtypeStruct((n_prompt * ROW_TILE, D_MODEL), F32),
                   jax.ShapeDtypeStruct((n_sample * ROW_TILE, D_MODEL), F32)],
        scratch_shapes=[pltpu.VMEM(wbg.shape, BF16), pltpu.VMEM(wba.shape, BF16), pltpu.VMEM(wo.shape, BF16),
                        pltpu.VMEM(win.shape, BF16), pltpu.VMEM(wout.shape, BF16)],
        compiler_params=_params("arbitrary"),
        name="mix_ffn2",
    )(h, oa, ob, p, p, wbg, wba, wo, g2, win, wout, gf)


def _kept_rows(p, row0, batch, t, col):
    keep = min(ATT_PAST, t)
    rows = p[row0:row0 + batch * t, col:col + ATT_WIDTH].reshape(batch, t, ATT_WIDTH)[:, t - keep:]
    return rows.astype(F32).reshape(1, batch, keep, ATT_HEADS, ATT_DH)


def kernel(x_prompt, x_sample, cache_att_k, cache_att_v, state_gla, norm_ffn1, w_ffn1_in, w_ffn1_out,
           norm_mix, w_in, w_gla_gate, b_gla_gate, gla_norm, attn_rel_bias, w_branch_gla, w_branch_att,
           w_out, norm_ffn2, w_ffn2_in, w_ffn2_out, norm_final):
    assert norm_ffn1.shape[0] == 1, "single layer"
    bp, tp, _ = x_prompt.shape
    bs, ts, _ = x_sample.shape
    mp, ms = bp * tp, bs * ts
    assert mp % ROW_TILE == 0 and ms % ROW_TILE == 0
    assert cache_att_k.shape[2] == ATT_PAST and ts <= ATT_TQ

    wg = jnp.concatenate([w_gla_gate[0], jnp.zeros((LANES - GLA_GATE_RANK, GLA_K_WIDTH), F32)],
                         axis=0).astype(BF16)
    rel = _rel_row(attn_rel_bias[0])

    h, n = _ffn1(x_prompt.reshape(mp, D_MODEL), x_sample.reshape(ms, D_MODEL), norm_ffn1, w_ffn1_in[0],
                 w_ffn1_out[0], norm_mix)
    p, k_prompt_t, v_prompt_t = _proj(n, w_in[0].T, bp, tp)
    to_rows_last = lambda a: jnp.transpose(a, (0, 1, 3, 4, 2))
    to_rows_first = lambda a: jnp.transpose(a, (0, 1, 4, 2, 3))
    gla_w = (wg, b_gla_gate, gla_norm)
    o_a, s_prompt = _gla(p, *gla_w, jnp.zeros((bp, GLA_HEADS, GLA_DK, GLA_DV), F32), bp, tp, 0)
    o_a, s_sample = _gla(p, *gla_w, state_gla[0], bs, ts, mp, o_a)
    o_b = _band_prompt(p, rel, bp, tp)
    o_b = _band_step(p, to_rows_last(cache_att_k), to_rows_last(cache_att_v), rel, bs, ts, mp, o_b)
    yp, ys = _mix_ffn2(h, o_a, o_b, p, w_branch_gla[0], w_branch_att[0], w_out[0], norm_ffn2, w_ffn2_in[0],
                       w_ffn2_out[0], norm_final.reshape(1, D_MODEL), mp // ROW_TILE)
    return (yp.reshape(bp, tp, D_MODEL), ys.reshape(bs, ts, D_MODEL), to_rows_first(k_prompt_t),
            to_rows_first(v_prompt_t), s_prompt[None],
            _kept_rows(p, mp, bs, ts, COL_KB), _kept_rows(p, mp, bs, ts, COL_VB), s_sample[None])
```

```python
import functools

import jax
import jax.numpy as jnp
import numpy as np
from jax import lax
from jax.experimental import pallas as pl
from jax.experimental.pallas import tpu as pltpu

F32 = jnp.float32
BF16 = jnp.bfloat16

D_MODEL = 1024
CHUNK = 64
GLA_HEADS = 4
GLA_DK = 128
GLA_DV = 256
GLA_K_WIDTH = GLA_HEADS * GLA_DK
GLA_V_WIDTH = GLA_HEADS * GLA_DV
GLA_GATE_RANK = 16
GLA_TAU = 16.0
ATT_HEADS = 16
ATT_DH = 64
ATT_WIDTH = ATT_HEADS * ATT_DH
ATT_PAST = 8 * CHUNK
REL_CLIP = 128
D_FF = 2816
EPS = 1e-6
NEG_INF = -1e30
LOG2E = 1.4426950408889634

LANES = 128
VMEM_LIMIT = 60 * 1024 * 1024

COL_QA = 0
COL_KA = COL_QA + GLA_K_WIDTH
COL_VA = COL_KA + GLA_K_WIDTH
COL_RA = COL_VA + GLA_V_WIDTH
COL_QB = COL_RA + GLA_V_WIDTH
COL_KB = COL_QB + ATT_WIDTH
COL_VB = COL_KB + ATT_WIDTH
COL_GA = COL_VB + ATT_WIDTH
COL_GB = COL_GA + D_MODEL
COL_FA = COL_GB + D_MODEL
PROJ_COLS = COL_FA + LANES


def _const_spec(shape):
    nd = len(shape)
    return pl.BlockSpec(shape, lambda *_: (0,) * nd, pipeline_mode=pl.Buffered(1))


def _params(*sem):
    return pltpu.CompilerParams(dimension_semantics=sem, vmem_limit_bytes=VMEM_LIMIT)


def _rms(x, g):
    return x * lax.rsqrt(jnp.mean(x * x, axis=-1, keepdims=True) + EPS) * g


def _swiglu(xn, win_ref, wout_ref):
    gate = jnp.dot(xn, win_ref[:, :D_FF], preferred_element_type=F32)
    up = jnp.dot(xn, win_ref[:, D_FF:], preferred_element_type=F32)
    act = (gate * jax.nn.sigmoid(gate) * up).astype(BF16)
    return jnp.dot(act, wout_ref[...], preferred_element_type=F32)


W_STEPS = 16
FFN_SLABS = 11


def _slab_spec(shape, axis, steps, **kw):
    block = tuple(d // steps if a == axis else d for a, d in enumerate(shape))
    assert block[axis] * steps == shape[axis] and steps <= W_STEPS
    if axis == 0:
        return pl.BlockSpec(block, lambda i: (jnp.minimum(i, steps - 1), 0), **kw)
    return pl.BlockSpec(block, lambda i: (0, jnp.minimum(i, steps - 1)), **kw)


def _stage(step, src_ref, dst_ref, axis):
    n = src_ref.shape[axis]
    for c in range(dst_ref.shape[axis] // n):
        @pl.when(step == c)
        def _(c=c):
            if axis == 0:
                dst_ref[c * n:(c + 1) * n, :] = src_ref[...].astype(BF16)
            else:
                dst_ref[:, c * n:(c + 1) * n] = src_ref[...].astype(BF16)


ROW_TILE = 512


def _row_tiles(first, count, cols=D_MODEL, col=0, **kw):
    return pl.BlockSpec((ROW_TILE, cols), lambda i: (jnp.clip(i - W_STEPS - first, 0, count - 1), col), **kw)


def _ffn1_kernel(xp_ref, xs_ref, g1_ref, win_ref, wout_ref, gmix_ref, h_ref, n_ref, win_s, wout_s, *,
                 n_prompt):
    step = pl.program_id(0)
    _stage(step, win_ref, win_s, 1)
    _stage(step, wout_ref, wout_s, 0)

    def tile(x_ref):
        x = x_ref[...]
        xn = _rms(x, g1_ref[...]).astype(BF16)
        h = x + 0.5 * _swiglu(xn, win_s, wout_s)
        h_ref[...] = h
        n_ref[...] = _rms(h, gmix_ref[...]).astype(BF16)

    @pl.when((step >= W_STEPS) & (step < W_STEPS + n_prompt))
    def _():
        tile(xp_ref)

    @pl.when(step >= W_STEPS + n_prompt)
    def _():
        tile(xs_ref)


def _ffn1(xp, xs, g1, win, wout, gmix):
    n_prompt, n_sample = xp.shape[0] // ROW_TILE, xs.shape[0] // ROW_TILE
    tiles = n_prompt + n_sample
    m = tiles * ROW_TILE
    return pl.pallas_call(
        functools.partial(_ffn1_kernel, n_prompt=n_prompt),
        grid=(W_STEPS + tiles,),
        in_specs=[_row_tiles(0, n_prompt), _row_tiles(n_prompt, n_sample, pipeline_mode=pl.Buffered(1)),
                  _const_spec(g1.shape), _slab_spec(win.shape, 1, FFN_SLABS),
                  _slab_spec(wout.shape, 0, FFN_SLABS), _const_spec(gmix.shape)],
        out_specs=[_row_tiles(0, tiles), _row_tiles(0, tiles)],
        out_shape=[jax.ShapeDtypeStruct((m, D_MODEL), F32), jax.ShapeDtypeStruct((m, D_MODEL), BF16)],
        scratch_shapes=[pltpu.VMEM(win.shape, BF16), pltpu.VMEM(wout.shape, BF16)],
        compiler_params=_params("arbitrary"),
        name="ffn1",
    )(xp, xs, g1, win, wout, gmix)


PROJ_STEP = 1024


W_IN_SLAB = 1024
W_IN_GATE0 = COL_QB


def _stage_w_in(step, src_ref, dst_ref, n_cols):
    n = src_ref.shape[0]
    g0, g1 = W_IN_GATE0, W_IN_GATE0 + GLA_GATE_RANK
    moves = [(0, g0, 0), (g0, g1, COL_FA), (g1, n_cols, g0)]
    for c in range(pl.cdiv(n_cols, n)):
        @pl.when(step == c)
        def _(c=c):
            lo, hi = c * n, min((c + 1) * n, n_cols)
            for first, end, new in moves:
                a, b = max(lo, first), min(hi, end)
                if a < b:
                    dst_ref[new + a - first:new + b - first, :] = src_ref[a - lo:b - lo, :].astype(BF16)
            if c == 0:
                pad = dst_ref.shape[0] - COL_FA - GLA_GATE_RANK
                dst_ref[COL_FA + GLA_GATE_RANK:, :] = jnp.zeros((pad, dst_ref.shape[1]), BF16)


def _proj_kernel(n_ref, w_ref, o_ref, kp_ref, vp_ref, w_s, *, n_prompt, seq_tiles, n_cols):
    step = pl.program_id(0)
    _stage_w_in(step, w_ref, w_s, n_cols)

    @pl.when(step >= W_STEPS)
    def _():
        tile = step - W_STEPS
        n = n_ref[...]
        kept = {}
        for c0 in range(0, PROJ_COLS, PROJ_STEP):
            c1 = min(c0 + PROJ_STEP, PROJ_COLS)
            acc = lax.dot_general(n, w_s[c0:c1, :], (((1,), (1,)), ((), ())), preferred_element_type=F32)
            o_ref[:, c0:c1] = acc.astype(BF16)
            if c0 in (COL_KB, COL_VB):
                kept[c0] = acc

        @pl.when((tile < n_prompt) & (tile % seq_tiles == seq_tiles - 1))
        def _():
            kp_ref[...] = kept[COL_KB].T.reshape(kp_ref.shape)
            vp_ref[...] = kept[COL_VB].T.reshape(vp_ref.shape)


def _proj(n, w_t, bp, tp):
    tiles = n.shape[0] // ROW_TILE
    n_prompt, seq_tiles = bp * tp // ROW_TILE, tp // ROW_TILE
    n_cols = w_t.shape[0]
    assert PROJ_STEP == ATT_WIDTH and min(ATT_PAST, tp) == ROW_TILE and pl.cdiv(n_cols, W_IN_SLAB) <= W_STEPS
    last_slab = pl.cdiv(n_cols, W_IN_SLAB) - 1
    kept = pl.BlockSpec((None, None, ATT_HEADS, ATT_DH, ROW_TILE),
                        lambda i: (0, jnp.clip((i - W_STEPS) // seq_tiles, 0, bp - 1), 0, 0, 0))
    return pl.pallas_call(
        functools.partial(_proj_kernel, n_prompt=n_prompt, seq_tiles=seq_tiles, n_cols=n_cols),
        grid=(W_STEPS + tiles,),
        in_specs=[_row_tiles(0, tiles),
                  pl.BlockSpec((W_IN_SLAB, D_MODEL), lambda i: (jnp.minimum(i, last_slab), 0))],
        out_specs=[_row_tiles(0, tiles, cols=PROJ_COLS), kept, kept],
        out_shape=[jax.ShapeDtypeStruct((tiles * ROW_TILE, PROJ_COLS), BF16)]
        + [jax.ShapeDtypeStruct((1, bp, ATT_HEADS, ATT_DH, ROW_TILE), F32)] * 2,
        scratch_shapes=[pltpu.VMEM((PROJ_COLS, D_MODEL), BF16)],
        compiler_params=_params("arbitrary"),
        name="proj",
    )(n, w_t)


GLA_ROWS = 2048


def _gla_kernel(q_ref, k_ref, v_ref, r_ref, f_ref, wg_ref, bg_ref, gn_ref, s0_ref, o_ref, s_ref, *, n_seq):
    cl = CHUNK
    nc = q_ref.shape[0] // cl
    per_seq = nc // n_seq
    bmm = functools.partial(jnp.einsum, preferred_element_type=F32)

    z = jnp.dot(f_ref[...], wg_ref[...], preferred_element_type=F32) + bg_ref[...]
    log_a = (jnp.minimum(z, 0.0) - jnp.log(1.0 + jnp.exp(-jnp.abs(z)))) * (LOG2E / GLA_TAU)
    log_a = log_a.reshape(nc, cl, GLA_DK)

    ri = lax.broadcasted_iota(jnp.int32, (cl, cl), 0)
    ci = lax.broadcasted_iota(jnp.int32, (cl, cl), 1)
    causal = ri >= ci
    tri = jnp.broadcast_to(causal.astype(BF16), (nc, cl, cl))
    hi = log_a.astype(BF16)
    lo = (log_a - hi.astype(F32)).astype(BF16)
    b = bmm('nij,njd->nid', tri, hi) + bmm('nij,njd->nid', tri, lo)
    b_last = b[:, cl - 1:cl, :]

    q = q_ref[...].astype(F32).reshape(nc, cl, GLA_DK) * (GLA_DK ** -0.5)
    k = k_ref[...].astype(F32).reshape(nc, cl, GLA_DK)
    v = v_ref[...].reshape(nc, cl, GLA_DV)
    q_dec = (q * jnp.exp2(b)).astype(BF16)
    k_inv = (k * jnp.exp2(-b)).astype(BF16)
    k_end = (k * jnp.exp2(b_last - b)).astype(BF16)

    scores = jnp.where(causal, bmm('nid,njd->nij', q_dec, k_inv), 0.0)
    o_intra = bmm('nij,njv->niv', scores.astype(BF16), v)
    incr_t = bmm('njv,njd->nvd', v, k_end)
    decay = jnp.exp2(b_last)

    before = []
    for i in range(n_seq):
        s = s0_ref[i].T
        for c in range(i * per_seq, (i + 1) * per_seq):
            before.append(s.astype(BF16))
            s = s * decay[c] + incr_t[c]
        s_ref[i] = s.T
    s_before = jnp.stack(before)
    o = o_intra + bmm('nid,nvd->niv', q_dec, s_before)

    o = o * lax.rsqrt(jnp.mean(o * o, axis=-1, keepdims=True) + EPS) * gn_ref[...]
    r = r_ref[...].astype(F32)
    o = o.reshape(nc * cl, GLA_DV) * (r * jax.nn.sigmoid(r))
    o_ref[...] = o.astype(BF16)


def _gla(p, wg, bg, gn, s0, batch, t, row0):
    hq, hk = COL_QA // GLA_DK, COL_KA // GLA_DK
    hv, hr = COL_VA // GLA_DV, COL_RA // GLA_DV
    n_seq = min(batch, max(1, GLA_ROWS // t))
    rows = n_seq * t
    assert row0 % rows == 0 and batch % n_seq == 0
    b0 = row0 // rows
    state = pl.BlockSpec((n_seq, None, GLA_DK, GLA_DV), lambda b, h: (b, h, 0, 0))
    return pl.pallas_call(
        functools.partial(_gla_kernel, n_seq=n_seq),
        grid=(batch // n_seq, GLA_HEADS),
        in_specs=[
            pl.BlockSpec((rows, GLA_DK), lambda b, h: (b0 + b, hq + h)),
            pl.BlockSpec((rows, GLA_DK), lambda b, h: (b0 + b, hk + h)),
            pl.BlockSpec((rows, GLA_DV), lambda b, h: (b0 + b, hv + h)),
            pl.BlockSpec((rows, GLA_DV), lambda b, h: (b0 + b, hr + h)),
            pl.BlockSpec((rows, LANES), lambda b, h: (b0 + b, COL_FA // LANES)),
            pl.BlockSpec((LANES, GLA_DK), lambda b, h: (0, h)),
            pl.BlockSpec((1, GLA_DK), lambda b, h: (0, h)),
            pl.BlockSpec((1, GLA_DV), lambda b, h: (0, h)),
            state,
        ],
        out_specs=[pl.BlockSpec((rows, GLA_DV), lambda b, h: (b, h)), state],
        out_shape=[jax.ShapeDtypeStruct((batch * t, GLA_V_WIDTH), BF16),
                   jax.ShapeDtypeStruct((batch, GLA_HEADS, GLA_DK, GLA_DV), F32)],
        compiler_params=_params("parallel", "parallel"),
        name="gla",
    )(p, p, p, p, p, wg, bg, gn, s0)


ATT_GROUP = 4
ATT_LANES = ATT_GROUP * ATT_DH
ATT_TQ = 256
REL_RING = 1024


def _rel_row(table):
    u = np.arange(REL_RING)
    u = np.where(u < ATT_PAST + ATT_TQ, u, u - REL_RING)
    idx = np.clip(ATT_PAST - u, -REL_CLIP, REL_CLIP) + REL_CLIP
    return table[:, idx].astype(F32)[:, None, :]


def _build_bias(rel_ref, bias_ref):
    _, tq, nk = bias_ref.shape
    qc = lax.broadcasted_iota(jnp.int32, (tq, nk), 0) // CHUNK
    kc = lax.broadcasted_iota(jnp.int32, (tq, nk), 1) // CHUNK
    visible = (kc >= qc) & (kc <= qc + ATT_PAST // CHUNK)
    for hh in range(bias_ref.shape[0]):
        row = jnp.broadcast_to(rel_ref[hh], (tq, REL_RING))
        toeplitz = pltpu.roll(row, 0, 1, stride=1, stride_axis=0)
        bias_ref[hh] = jnp.where(visible, toeplitz[:, :nk] * LOG2E, NEG_INF)


def _head_lanes(hh):
    lane = lax.broadcasted_iota(jnp.int32, (1, ATT_LANES), 1)
    return (lane >= hh * ATT_DH) & (lane < (hh + 1) * ATT_DH)


def _head_values(v):
    return [jnp.where(_head_lanes(hh), v, jnp.ones_like(v)) for hh in range(ATT_GROUP)]


def _attend(q, k, vhs, bias_ref, boff):
    nk = k.shape[0]
    q = q.astype(F32) * (ATT_DH ** -0.5 * LOG2E)
    out = None
    for hh in range(ATT_GROUP):
        mine = _head_lanes(hh)
        qh = jnp.where(mine, q, 0.0).astype(BF16)
        s = lax.dot_general(qh, k, (((1,), (1,)), ((), ())), preferred_element_type=F32)
        s = s + bias_ref[hh, :, boff:boff + nk]
        e = jnp.exp2(s - jnp.max(s, axis=-1, keepdims=True)).astype(BF16)
        pv = jnp.dot(e, vhs[hh], preferred_element_type=F32)
        o = pv / pltpu.roll(pv, ATT_DH, 1)
        out = o if out is None else jnp.where(mine, o, out)
    return out.astype(BF16)


def _band_prompt_kernel(rel_ref, q_ref, k_ref, v_ref, o_ref, bias_ref):
    _, tq, nk = bias_ref.shape
    n_blocks = q_ref.shape[0] // tq
    n_past = ATT_PAST // tq

    @pl.when(pl.program_id(1) == 0)
    def _():
        _build_bias(rel_ref, bias_ref)

    vhs = _head_values(v_ref[...])
    for tt in range(n_blocks):
        rows = slice(tt * tq, (tt + 1) * tq)
        keys = slice(max(tt - n_past, 0) * tq, (tt + 1) * tq)
        o_ref[rows, :] = _attend(q_ref[rows, :], k_ref[keys, :], [vh[keys, :] for vh in vhs], bias_ref,
                                 max(n_past - tt, 0) * tq)


def _band_prompt(p, rel, batch, t):
    w = ATT_LANES
    cq, ck, cv = COL_QB // w, COL_KB // w, COL_VB // w
    return pl.pallas_call(
        _band_prompt_kernel,
        grid=(ATT_HEADS // ATT_GROUP, batch),
        in_specs=[
            pl.BlockSpec((ATT_GROUP, 1, REL_RING), lambda g, b: (g, 0, 0)),
            pl.BlockSpec((t, w), lambda g, b: (b, cq + g)),
            pl.BlockSpec((t, w), lambda g, b: (b, ck + g)),
            pl.BlockSpec((t, w), lambda g, b: (b, cv + g)),
        ],
        out_specs=pl.BlockSpec((t, w), lambda g, b: (b, g)),
        out_shape=jax.ShapeDtypeStruct((batch * t, ATT_WIDTH), BF16),
        scratch_shapes=[pltpu.VMEM((ATT_GROUP, ATT_TQ, ATT_PAST + ATT_TQ), F32)],
        compiler_params=_params("arbitrary", "arbitrary"),
        name="band_prompt",
    )(rel, p, p, p)


def _attend_cached(q, kc_t, vc_t, k_new, v_new, bias):
    tq, c = q.shape[0], kc_t.shape[1]
    nt = (((1,), (1,)), ((), ()))
    q = q.astype(F32) * (ATT_DH ** -0.5 * LOG2E)
    qs = jnp.concatenate([jnp.where(_head_lanes(hh), q, 0.0) for hh in range(ATT_GROUP)], axis=0).astype(BF16)
    s = jnp.concatenate([jnp.dot(qs, kc_t, preferred_element_type=F32),
                         lax.dot_general(qs, k_new, nt, preferred_element_type=F32)], axis=1) + bias
    e = jnp.exp2(s - jnp.max(s, axis=-1, keepdims=True))
    eb = e.astype(BF16)
    pv = (lax.dot_general(eb[:, :c], vc_t, nt, preferred_element_type=F32)
          + jnp.dot(eb[:, c:], v_new, preferred_element_type=F32)) / jnp.sum(e, axis=-1, keepdims=True)
    out = pv[:tq]
    for hh in range(1, ATT_GROUP):
        out = jnp.where(_head_lanes(hh), pv[hh * tq:(hh + 1) * tq], out)
    return out.astype(BF16)


def _band_step_kernel(rel_ref, q_ref, k_ref, v_ref, ck_ref, cv_ref, o_ref, bias_ref):
    @pl.when(pl.program_id(0) == 0)
    def _():
        _build_bias(rel_ref, bias_ref)

    t, nk = bias_ref.shape[1:]
    c = ck_ref.shape[2]
    kc_t = ck_ref[...].reshape(ATT_WIDTH, c).astype(BF16)
    vc_t = cv_ref[...].reshape(ATT_WIDTH, c).astype(BF16)
    for g in range(ATT_HEADS // ATT_GROUP):
        lanes = slice(g * ATT_LANES, (g + 1) * ATT_LANES)
        bias = bias_ref[g * ATT_GROUP:(g + 1) * ATT_GROUP].reshape(ATT_GROUP * t, nk)
        o_ref[:, lanes] = _attend_cached(q_ref[:, lanes], kc_t[lanes, :], vc_t[lanes, :], k_ref[:, lanes],
                                         v_ref[:, lanes], bias)


def _band_step(p, cache_k_t, cache_v_t, rel, batch, t, row0):
    c = cache_k_t.shape[4]
    assert row0 % t == 0
    r0 = row0 // t
    new = lambda col: pl.BlockSpec((t, ATT_WIDTH), lambda b: (r0 + b, col // ATT_WIDTH))
    cache = pl.BlockSpec((None, None, ATT_HEADS, ATT_DH, c), lambda b: (0, b, 0, 0, 0))
    return pl.pallas_call(
        _band_step_kernel,
        grid=(batch,),
        in_specs=[_const_spec(rel.shape), new(COL_QB), new(COL_KB), new(COL_VB), cache, cache],
        out_specs=pl.BlockSpec((t, ATT_WIDTH), lambda b: (b, 0)),
        out_shape=jax.ShapeDtypeStruct((batch * t, ATT_WIDTH), BF16),
        scratch_shapes=[pltpu.VMEM((ATT_HEADS, t, c + t), F32)],
        compiler_params=_params("arbitrary"),
        name="band_step",
    )(rel, p, p, p, cache_k_t, cache_v_t)


SQUARE_STEPS = 8


def _mix_ffn2_kernel(h_ref, oap_ref, obp_ref, oas_ref, obs_ref, ga_ref, gb_ref, wbg_ref, wba_ref, wo_ref,
                     g2_ref, win_ref, wout_ref, gf_ref, yp_ref, ys_ref, wbg_s, wba_s, wo_s, win_s, wout_s,
                     *, n_prompt):
    step = pl.program_id(0)
    _stage(step, wbg_ref, wbg_s, 0)
    _stage(step, wba_ref, wba_s, 0)
    _stage(step, wo_ref, wo_s, 0)
    _stage(step, win_ref, win_s, 1)
    _stage(step, wout_ref, wout_s, 0)

    def tile(oa_ref, ob_ref, y_ref):
        a = jnp.dot(oa_ref[...], wbg_s[...], preferred_element_type=F32)
        b = jnp.dot(ob_ref[...], wba_s[...], preferred_element_type=F32)
        mixed = (jax.nn.sigmoid(ga_ref[...].astype(F32)) * a
                 + jax.nn.sigmoid(gb_ref[...].astype(F32)) * b)
        h = h_ref[...] + jnp.dot(mixed.astype(BF16), wo_s[...], preferred_element_type=F32)
        xn = _rms(h, g2_ref[...]).astype(BF16)
        h = h + 0.5 * _swiglu(xn, win_s, wout_s)
        y_ref[...] = _rms(h, gf_ref[...])

    @pl.when((step >= W_STEPS) & (step < W_STEPS + n_prompt))
    def _():
        tile(oap_ref, obp_ref, yp_ref)

    @pl.when(step >= W_STEPS + n_prompt)
    def _():
        tile(oas_ref, obs_ref, ys_ref)


def _mix_ffn2(h, oa_p, ob_p, oa_s, ob_s, p, wbg, wba, wo, g2, win, wout, gf):
    tiles = h.shape[0] // ROW_TILE
    n_prompt, n_sample = oa_p.shape[0] // ROW_TILE, oa_s.shape[0] // ROW_TILE
    assert tiles == n_prompt + n_sample
    once = dict(pipeline_mode=pl.Buffered(1))
    prompt, sample = _row_tiles(0, n_prompt), _row_tiles(n_prompt, n_sample, **once)
    square = _slab_spec(wbg.shape, 0, SQUARE_STEPS, **once)
    return pl.pallas_call(
        functools.partial(_mix_ffn2_kernel, n_prompt=n_prompt),
        grid=(W_STEPS + tiles,),
        in_specs=[_row_tiles(0, tiles), prompt, prompt, sample, sample,
                  _row_tiles(0, tiles, col=COL_GA // D_MODEL), _row_tiles(0, tiles, col=COL_GB // D_MODEL),
                  square, square, square, _const_spec(g2.shape), _slab_spec(win.shape, 1, FFN_SLABS),
                  _slab_spec(wout.shape, 0, FFN_SLABS), _const_spec(gf.shape)],
        out_specs=[prompt, sample],
        out_shape=[jax.ShapeDtypeStruct((n_prompt * ROW_TILE, D_MODEL), F32),
                   jax.ShapeDtypeStruct((n_sample * ROW_TILE, D_MODEL), F32)],
        scratch_shapes=[pltpu.VMEM(wbg.shape, BF16), pltpu.VMEM(wba.shape, BF16), pltpu.VMEM(wo.shape, BF16),
                        pltpu.VMEM(win.shape, BF16), pltpu.VMEM(wout.shape, BF16)],
        compiler_params=_params("arbitrary"),
        name="mix_ffn2",
    )(h, oa_p, ob_p, oa_s, ob_s, p, p, wbg, wba, wo, g2, win, wout, gf)


def _kept_rows(p, row0, batch, t, col):
    keep = min(ATT_PAST, t)
    rows = p[row0:row0 + batch * t, col:col + ATT_WIDTH].reshape(batch, t, ATT_WIDTH)[:, t - keep:]
    return rows.astype(F32).reshape(1, batch, keep, ATT_HEADS, ATT_DH)


def kernel(x_prompt, x_sample, cache_att_k, cache_att_v, state_gla, norm_ffn1, w_ffn1_in, w_ffn1_out,
           norm_mix, w_in, w_gla_gate, b_gla_gate, gla_norm, attn_rel_bias, w_branch_gla, w_branch_att,
           w_out, norm_ffn2, w_ffn2_in, w_ffn2_out, norm_final):
    assert norm_ffn1.shape[0] == 1, "single layer"
    bp, tp, _ = x_prompt.shape
    bs, ts, _ = x_sample.shape
    mp, ms = bp * tp, bs * ts
    assert mp % ROW_TILE == 0 and ms % ROW_TILE == 0
    assert cache_att_k.shape[2] == ATT_PAST and ts <= ATT_TQ

    wg = jnp.concatenate([w_gla_gate[0], jnp.zeros((LANES - GLA_GATE_RANK, GLA_K_WIDTH), F32)],
                         axis=0).astype(BF16)
    rel = _rel_row(attn_rel_bias[0])

    h, n = _ffn1(x_prompt.reshape(mp, D_MODEL), x_sample.reshape(ms, D_MODEL), norm_ffn1, w_ffn1_in[0],
                 w_ffn1_out[0], norm_mix)
    p, k_prompt_t, v_prompt_t = _proj(n, w_in[0].T, bp, tp)
    to_rows_last = lambda a: jnp.transpose(a, (0, 1, 3, 4, 2))
    to_rows_first = lambda a: jnp.transpose(a, (0, 1, 4, 2, 3))
    gla_w = (wg, b_gla_gate, gla_norm)
    oa_p, s_prompt = _gla(p, *gla_w, jnp.zeros((bp, GLA_HEADS, GLA_DK, GLA_DV), F32), bp, tp, 0)
    oa_s, s_sample = _gla(p, *gla_w, state_gla[0], bs, ts, mp)
    ob_p = _band_prompt(p, rel, bp, tp)
    ob_s = _band_step(p, to_rows_last(cache_att_k), to_rows_last(cache_att_v), rel, bs, ts, mp)
    yp, ys = _mix_ffn2(h, oa_p, ob_p, oa_s, ob_s, p, w_branch_gla[0], w_branch_att[0], w_out[0], norm_ffn2,
                       w_ffn2_in[0], w_ffn2_out[0], norm_final.reshape(1, D_MODEL))
    return (yp.reshape(bp, tp, D_MODEL), ys.reshape(bs, ts, D_MODEL), to_rows_first(k_prompt_t),
            to_rows_first(v_prompt_t), s_prompt[None],
            _kept_rows(p, mp, bs, ts, COL_KB), _kept_rows(p, mp, bs, ts, COL_VB), s_sample[None])
```

```python
import functools

import jax
import jax.numpy as jnp
import numpy as np
from jax import lax
from jax.experimental import pallas as pl
from jax.experimental.pallas import tpu as pltpu

F32 = jnp.float32
BF16 = jnp.bfloat16

D_MODEL = 1024
CHUNK = 64
GLA_HEADS = 4
GLA_DK = 128
GLA_DV = 256
GLA_K_WIDTH = GLA_HEADS * GLA_DK
GLA_V_WIDTH = GLA_HEADS * GLA_DV
GLA_GATE_RANK = 16
GLA_TAU = 16.0
ATT_HEADS = 16
ATT_DH = 64
ATT_WIDTH = ATT_HEADS * ATT_DH
ATT_PAST = 8 * CHUNK
REL_CLIP = 128
D_FF = 2816
EPS = 1e-6
NEG_INF = -1e30
LOG2E = 1.4426950408889634

LANES = 128
VMEM_LIMIT = 60 * 1024 * 1024

COL_QA = 0
COL_KA = COL_QA + GLA_K_WIDTH
COL_VA = COL_KA + GLA_K_WIDTH
COL_RA = COL_VA + GLA_V_WIDTH
COL_QB = COL_RA + GLA_V_WIDTH
COL_KB = COL_QB + ATT_WIDTH
COL_VB = COL_KB + ATT_WIDTH
COL_GA = COL_VB + ATT_WIDTH
COL_GB = COL_GA + D_MODEL
COL_FA = COL_GB + D_MODEL
PROJ_COLS = COL_FA + LANES


def _const_spec(shape):
    nd = len(shape)
    return pl.BlockSpec(shape, lambda *_: (0,) * nd, pipeline_mode=pl.Buffered(1))


def _params(*sem):
    return pltpu.CompilerParams(dimension_semantics=sem, vmem_limit_bytes=VMEM_LIMIT)


def _rms(x, g):
    return x * lax.rsqrt(jnp.mean(x * x, axis=-1, keepdims=True) + EPS) * g


def _swiglu(xn, win_ref, wout_ref):
    gate = jnp.dot(xn, win_ref[:, :D_FF], preferred_element_type=F32)
    up = jnp.dot(xn, win_ref[:, D_FF:], preferred_element_type=F32)
    act = (gate * jax.nn.sigmoid(gate) * up).astype(BF16)
    return jnp.dot(act, wout_ref[...], preferred_element_type=F32)


W_STEPS = 16
FFN_SLABS = 11


def _slab_spec(shape, axis, steps, **kw):
    block = tuple(d // steps if a == axis else d for a, d in enumerate(shape))
    assert block[axis] * steps == shape[axis] and steps <= W_STEPS
    if axis == 0:
        return pl.BlockSpec(block, lambda i: (jnp.minimum(i, steps - 1), 0), **kw)
    return pl.BlockSpec(block, lambda i: (0, jnp.minimum(i, steps - 1)), **kw)


def _stage(step, src_ref, dst_ref, axis):
    n = src_ref.shape[axis]
    for c in range(dst_ref.shape[axis] // n):
        @pl.when(step == c)
        def _(c=c):
            if axis == 0:
                dst_ref[c * n:(c + 1) * n, :] = src_ref[...].astype(BF16)
            else:
                dst_ref[:, c * n:(c + 1) * n] = src_ref[...].astype(BF16)


ROW_TILE = 512


def _row_tiles(first, count, cols=D_MODEL, col=0, **kw):
    return pl.BlockSpec((ROW_TILE, cols), lambda i: (jnp.clip(i - W_STEPS - first, 0, count - 1), col), **kw)


def _ffn1_kernel(xp_ref, xs_ref, g1_ref, win_ref, wout_ref, gmix_ref, h_ref, n_ref, win_s, wout_s, *,
                 n_prompt):
    step = pl.program_id(0)
    _stage(step, win_ref, win_s, 1)
    _stage(step, wout_ref, wout_s, 0)

    def tile(x_ref):
        x = x_ref[...]
        xn = _rms(x, g1_ref[...]).astype(BF16)
        h = x + 0.5 * _swiglu(xn, win_s, wout_s)
        h_ref[...] = h
        n_ref[...] = _rms(h, gmix_ref[...]).astype(BF16)

    @pl.when((step >= W_STEPS) & (step < W_STEPS + n_prompt))
    def _():
        tile(xp_ref)

    @pl.when(step >= W_STEPS + n_prompt)
    def _():
        tile(xs_ref)


def _ffn1(xp, xs, g1, win, wout, gmix):
    n_prompt, n_sample = xp.shape[0] // ROW_TILE, xs.shape[0] // ROW_TILE
    tiles = n_prompt + n_sample
    m = tiles * ROW_TILE
    return pl.pallas_call(
        functools.partial(_ffn1_kernel, n_prompt=n_prompt),
        grid=(W_STEPS + tiles,),
        in_specs=[_row_tiles(0, n_prompt), _row_tiles(n_prompt, n_sample, pipeline_mode=pl.Buffered(1)),
                  _const_spec(g1.shape), _slab_spec(win.shape, 1, FFN_SLABS),
                  _slab_spec(wout.shape, 0, FFN_SLABS), _const_spec(gmix.shape)],
        out_specs=[_row_tiles(0, tiles), _row_tiles(0, tiles)],
        out_shape=[jax.ShapeDtypeStruct((m, D_MODEL), F32), jax.ShapeDtypeStruct((m, D_MODEL), BF16)],
        scratch_shapes=[pltpu.VMEM(win.shape, BF16), pltpu.VMEM(wout.shape, BF16)],
        compiler_params=_params("arbitrary"),
        name="ffn1",
    )(xp, xs, g1, win, wout, gmix)


PROJ_STEP = 1024


W_IN_SLAB = 1024
W_IN_GATE0 = COL_QB


def _stage_w_in(step, src_ref, dst_ref, n_cols):
    n = src_ref.shape[0]
    g0, g1 = W_IN_GATE0, W_IN_GATE0 + GLA_GATE_RANK
    moves = [(0, g0, 0), (g0, g1, COL_FA), (g1, n_cols, g0)]
    for c in range(pl.cdiv(n_cols, n)):
        @pl.when(step == c)
        def _(c=c):
            lo, hi = c * n, min((c + 1) * n, n_cols)
            for first, end, new in moves:
                a, b = max(lo, first), min(hi, end)
                if a < b:
                    dst_ref[new + a - first:new + b - first, :] = src_ref[a - lo:b - lo, :].astype(BF16)
            if c == 0:
                pad = dst_ref.shape[0] - COL_FA - GLA_GATE_RANK
                dst_ref[COL_FA + GLA_GATE_RANK:, :] = jnp.zeros((pad, dst_ref.shape[1]), BF16)


def _proj_kernel(n_ref, w_ref, o_ref, vt_ref, kp_ref, vp_ref, w_s, *, n_prompt, seq_tiles, n_cols):
    step = pl.program_id(0)
    _stage_w_in(step, w_ref, w_s, n_cols)

    @pl.when(step >= W_STEPS)
    def _():
        tile = step - W_STEPS
        n = n_ref[...]
        kept = {}
        for c0 in range(0, PROJ_COLS, PROJ_STEP):
            c1 = min(c0 + PROJ_STEP, PROJ_COLS)
            acc = lax.dot_general(n, w_s[c0:c1, :], (((1,), (1,)), ((), ())), preferred_element_type=F32)
            o_ref[:, c0:c1] = acc.astype(BF16)
            if c0 in (COL_KB, COL_VB):
                kept[c0] = acc
        v_t = kept[COL_VB].T
        vt_ref[...] = v_t.astype(BF16)

        @pl.when((tile < n_prompt) & (tile % seq_tiles == seq_tiles - 1))
        def _():
            kp_ref[...] = kept[COL_KB].T.reshape(kp_ref.shape)
            vp_ref[...] = v_t.reshape(vp_ref.shape)


def _proj(n, w_t, bp, tp):
    tiles = n.shape[0] // ROW_TILE
    n_prompt, seq_tiles = bp * tp // ROW_TILE, tp // ROW_TILE
    n_cols = w_t.shape[0]
    assert PROJ_STEP == ATT_WIDTH and min(ATT_PAST, tp) == ROW_TILE and pl.cdiv(n_cols, W_IN_SLAB) <= W_STEPS
    last_slab = pl.cdiv(n_cols, W_IN_SLAB) - 1
    kept = pl.BlockSpec((None, None, ATT_HEADS, ATT_DH, ROW_TILE),
                        lambda i: (0, jnp.clip((i - W_STEPS) // seq_tiles, 0, bp - 1), 0, 0, 0))
    return pl.pallas_call(
        functools.partial(_proj_kernel, n_prompt=n_prompt, seq_tiles=seq_tiles, n_cols=n_cols),
        grid=(W_STEPS + tiles,),
        in_specs=[_row_tiles(0, tiles),
                  pl.BlockSpec((W_IN_SLAB, D_MODEL), lambda i: (jnp.minimum(i, last_slab), 0))],
        out_specs=[_row_tiles(0, tiles, cols=PROJ_COLS),
                   pl.BlockSpec((ATT_WIDTH, ROW_TILE), lambda i: (0, jnp.clip(i - W_STEPS, 0, tiles - 1))),
                   kept, kept],
        out_shape=[jax.ShapeDtypeStruct((tiles * ROW_TILE, PROJ_COLS), BF16),
                   jax.ShapeDtypeStruct((ATT_WIDTH, tiles * ROW_TILE), BF16)]
        + [jax.ShapeDtypeStruct((1, bp, ATT_HEADS, ATT_DH, ROW_TILE), F32)] * 2,
        scratch_shapes=[pltpu.VMEM((PROJ_COLS, D_MODEL), BF16)],
        compiler_params=_params("arbitrary"),
        name="proj",
    )(n, w_t)


GLA_ROWS = 2048


def _gla_kernel(q_ref, k_ref, v_ref, r_ref, f_ref, wg_ref, bg_ref, gn_ref, s0_ref, o_ref, s_ref, *, n_seq):
    cl = CHUNK
    nc = q_ref.shape[0] // cl
    per_seq = nc // n_seq
    bmm = functools.partial(jnp.einsum, preferred_element_type=F32)

    z = jnp.dot(f_ref[...], wg_ref[...], preferred_element_type=F32) + bg_ref[...]
    log_a = (jnp.minimum(z, 0.0) - jnp.log(1.0 + jnp.exp(-jnp.abs(z)))) * (LOG2E / GLA_TAU)
    log_a = log_a.reshape(nc, cl, GLA_DK)

    ri = lax.broadcasted_iota(jnp.int32, (cl, cl), 0)
    ci = lax.broadcasted_iota(jnp.int32, (cl, cl), 1)
    causal = ri >= ci
    tri = jnp.broadcast_to(causal.astype(BF16), (nc, cl, cl))
    hi = log_a.astype(BF16)
    lo = (log_a - hi.astype(F32)).astype(BF16)
    b = bmm('nij,njd->nid', tri, hi) + bmm('nij,njd->nid', tri, lo)
    b_last = b[:, cl - 1:cl, :]

    q = q_ref[...].astype(F32).reshape(nc, cl, GLA_DK) * (GLA_DK ** -0.5)
    k = k_ref[...].astype(F32).reshape(nc, cl, GLA_DK)
    v = v_ref[...].reshape(nc, cl, GLA_DV)
    q_dec = (q * jnp.exp2(b)).astype(BF16)
    k_inv = (k * jnp.exp2(-b)).astype(BF16)
    k_end = (k * jnp.exp2(b_last - b)).astype(BF16)

    scores = jnp.where(causal, bmm('nid,njd->nij', q_dec, k_inv), 0.0)
    o_intra = bmm('nij,njv->niv', scores.astype(BF16), v)
    incr_t = bmm('njv,njd->nvd', v, k_end)
    decay = jnp.exp2(b_last)

    before = []
    for i in range(n_seq):
        s = s0_ref[i].T
        for c in range(i * per_seq, (i + 1) * per_seq):
            before.append(s.astype(BF16))
            s = s * decay[c] + incr_t[c]
        s_ref[i] = s.T
    s_before = jnp.stack(before)
    o = o_intra + bmm('nid,nvd->niv', q_dec, s_before)

    o = o * lax.rsqrt(jnp.mean(o * o, axis=-1, keepdims=True) + EPS) * gn_ref[...]
    r = r_ref[...].astype(F32)
    o = o.reshape(nc * cl, GLA_DV) * (r * jax.nn.sigmoid(r))
    o_ref[...] = o.astype(BF16)


def _gla(p, wg, bg, gn, s0, batch, t, row0):
    hq, hk = COL_QA // GLA_DK, COL_KA // GLA_DK
    hv, hr = COL_VA // GLA_DV, COL_RA // GLA_DV
    n_seq = min(batch, max(1, GLA_ROWS // t))
    rows = n_seq * t
    assert row0 % rows == 0 and batch % n_seq == 0
    b0 = row0 // rows
    state = pl.BlockSpec((n_seq, None, GLA_DK, GLA_DV), lambda b, h: (b, h, 0, 0))
    return pl.pallas_call(
        functools.partial(_gla_kernel, n_seq=n_seq),
        grid=(batch // n_seq, GLA_HEADS),
        in_specs=[
            pl.BlockSpec((rows, GLA_DK), lambda b, h: (b0 + b, hq + h)),
            pl.BlockSpec((rows, GLA_DK), lambda b, h: (b0 + b, hk + h)),
            pl.BlockSpec((rows, GLA_DV), lambda b, h: (b0 + b, hv + h)),
            pl.BlockSpec((rows, GLA_DV), lambda b, h: (b0 + b, hr + h)),
            pl.BlockSpec((rows, LANES), lambda b, h: (b0 + b, COL_FA // LANES)),
            pl.BlockSpec((LANES, GLA_DK), lambda b, h: (0, h)),
            pl.BlockSpec((1, GLA_DK), lambda b, h: (0, h)),
            pl.BlockSpec((1, GLA_DV), lambda b, h: (0, h)),
            state,
        ],
        out_specs=[pl.BlockSpec((rows, GLA_DV), lambda b, h: (b, h)), state],
        out_shape=[jax.ShapeDtypeStruct((batch * t, GLA_V_WIDTH), BF16),
                   jax.ShapeDtypeStruct((batch, GLA_HEADS, GLA_DK, GLA_DV), F32)],
        compiler_params=_params("parallel", "parallel"),
        name="gla",
    )(p, p, p, p, p, wg, bg, gn, s0)


ATT_GROUP = 4
ATT_LANES = ATT_GROUP * ATT_DH
ATT_TQ = 256
REL_RING = 1024


def _rel_row(table):
    u = np.arange(REL_RING)
    u = np.where(u < ATT_PAST + ATT_TQ, u, u - REL_RING)
    idx = np.clip(ATT_PAST - u, -REL_CLIP, REL_CLIP) + REL_CLIP
    return table[:, idx].astype(F32)[:, None, :]


def _build_bias(rel_ref, bias_ref):
    _, tq, nk = bias_ref.shape
    qc = lax.broadcasted_iota(jnp.int32, (tq, nk), 0) // CHUNK
    kc = lax.broadcasted_iota(jnp.int32, (tq, nk), 1) // CHUNK
    visible = (kc >= qc) & (kc <= qc + ATT_PAST // CHUNK)
    for hh in range(bias_ref.shape[0]):
        row = jnp.broadcast_to(rel_ref[hh], (tq, REL_RING))
        toeplitz = pltpu.roll(row, 0, 1, stride=1, stride_axis=0)
        bias_ref[hh] = jnp.where(visible, toeplitz[:, :nk] * LOG2E, NEG_INF)


def _head_lanes(hh):
    lane = lax.broadcasted_iota(jnp.int32, (1, ATT_LANES), 1)
    return (lane >= hh * ATT_DH) & (lane < (hh + 1) * ATT_DH)


SUM_ROWS = 16


def _head_values_t(v_t):
    ones = jnp.ones((SUM_ROWS, v_t.shape[1]), v_t.dtype)
    return [jnp.concatenate([v_t[hh * ATT_DH:(hh + 1) * ATT_DH, :], ones], axis=0) for hh in range(ATT_GROUP)]


def _attend(q, k, vhs_t, bias_ref, boff):
    nk = k.shape[0]
    nt = (((1,), (1,)), ((), ()))
    q = q.astype(F32) * (ATT_DH ** -0.5 * LOG2E)
    outs_t = []
    for hh in range(ATT_GROUP):
        qh = jnp.where(_head_lanes(hh), q, 0.0).astype(BF16)
        s = lax.dot_general(qh, k, nt, preferred_element_type=F32) + bias_ref[hh, :, boff:boff + nk]
        e = jnp.exp2(s - jnp.max(s, axis=-1, keepdims=True)).astype(BF16)
        pv_t = lax.dot_general(vhs_t[hh], e, nt, preferred_element_type=F32)
        outs_t.append(pv_t[:ATT_DH] / pv_t[ATT_DH:ATT_DH + 1])
    return jnp.concatenate(outs_t, axis=0).T.astype(BF16)


def _band_prompt_kernel(rel_ref, q_ref, k_ref, vt_ref, o_ref, bias_ref):
    _, tq, nk = bias_ref.shape
    n_blocks = q_ref.shape[0] // tq
    n_past = ATT_PAST // tq

    @pl.when(pl.program_id(1) == 0)
    def _():
        _build_bias(rel_ref, bias_ref)

    vhs_t = _head_values_t(vt_ref[...])
    for tt in range(n_blocks):
        rows = slice(tt * tq, (tt + 1) * tq)
        keys = slice(max(tt - n_past, 0) * tq, (tt + 1) * tq)
        o_ref[rows, :] = _attend(q_ref[rows, :], k_ref[keys, :], [vh[:, keys] for vh in vhs_t], bias_ref,
                                 max(n_past - tt, 0) * tq)


def _band_prompt(p, v_t, rel, batch, t):
    w = ATT_LANES
    cq, ck = COL_QB // w, COL_KB // w
    return pl.pallas_call(
        _band_prompt_kernel,
        grid=(ATT_HEADS // ATT_GROUP, batch),
        in_specs=[
            pl.BlockSpec((ATT_GROUP, 1, REL_RING), lambda g, b: (g, 0, 0)),
            pl.BlockSpec((t, w), lambda g, b: (b, cq + g)),
            pl.BlockSpec((t, w), lambda g, b: (b, ck + g)),
            pl.BlockSpec((w, t), lambda g, b: (g, b)),
        ],
        out_specs=pl.BlockSpec((t, w), lambda g, b: (b, g)),
        out_shape=jax.ShapeDtypeStruct((batch * t, ATT_WIDTH), BF16),
        scratch_shapes=[pltpu.VMEM((ATT_GROUP, ATT_TQ, ATT_PAST + ATT_TQ), F32)],
        compiler_params=_params("arbitrary", "arbitrary"),
        name="band_prompt",
    )(rel, p, p, v_t)


def _attend_cached(q, kc_t, vc_t, k_new, v_new, bias):
    tq, c = q.shape[0], kc_t.shape[1]
    nt = (((1,), (1,)), ((), ()))
    q = q.astype(F32) * (ATT_DH ** -0.5 * LOG2E)
    qs = jnp.concatenate([jnp.where(_head_lanes(hh), q, 0.0) for hh in range(ATT_GROUP)], axis=0).astype(BF16)
    s = jnp.concatenate([jnp.dot(qs, kc_t, preferred_element_type=F32),
                         lax.dot_general(qs, k_new, nt, preferred_element_type=F32)], axis=1) + bias
    e = jnp.exp2(s - jnp.max(s, axis=-1, keepdims=True))
    eb = e.astype(BF16)
    pv = (lax.dot_general(eb[:, :c], vc_t, nt, preferred_element_type=F32)
          + jnp.dot(eb[:, c:], v_new, preferred_element_type=F32)) / jnp.sum(e, axis=-1, keepdims=True)
    out = pv[:tq]
    for hh in range(1, ATT_GROUP):
        out = jnp.where(_head_lanes(hh), pv[hh * tq:(hh + 1) * tq], out)
    return out.astype(BF16)


def _band_step_kernel(rel_ref, q_ref, k_ref, v_ref, ck_ref, cv_ref, o_ref, bias_ref):
    @pl.when(pl.program_id(0) == 0)
    def _():
        _build_bias(rel_ref, bias_ref)

    t, nk = bias_ref.shape[1:]
    c = ck_ref.shape[2]
    kc_t = ck_ref[...].reshape(ATT_WIDTH, c).astype(BF16)
    vc_t = cv_ref[...].reshape(ATT_WIDTH, c).astype(BF16)
    for g in range(ATT_HEADS // ATT_GROUP):
        lanes = slice(g * ATT_LANES, (g + 1) * ATT_LANES)
        bias = bias_ref[g * ATT_GROUP:(g + 1) * ATT_GROUP].reshape(ATT_GROUP * t, nk)
        o_ref[:, lanes] = _attend_cached(q_ref[:, lanes], kc_t[lanes, :], vc_t[lanes, :], k_ref[:, lanes],
                                         v_ref[:, lanes], bias)


def _band_step(p, cache_k_t, cache_v_t, rel, batch, t, row0):
    c = cache_k_t.shape[4]
    assert row0 % t == 0
    r0 = row0 // t
    new = lambda col: pl.BlockSpec((t, ATT_WIDTH), lambda b: (r0 + b, col // ATT_WIDTH))
    cache = pl.BlockSpec((None, None, ATT_HEADS, ATT_DH, c), lambda b: (0, b, 0, 0, 0))
    return pl.pallas_call(
        _band_step_kernel,
        grid=(batch,),
        in_specs=[_const_spec(rel.shape), new(COL_QB), new(COL_KB), new(COL_VB), cache, cache],
        out_specs=pl.BlockSpec((t, ATT_WIDTH), lambda b: (b, 0)),
        out_shape=jax.ShapeDtypeStruct((batch * t, ATT_WIDTH), BF16),
        scratch_shapes=[pltpu.VMEM((ATT_HEADS, t, c + t), F32)],
        compiler_params=_params("arbitrary"),
        name="band_step",
    )(rel, p, p, p, cache_k_t, cache_v_t)


SQUARE_STEPS = 8


def _mix_ffn2_kernel(h_ref, oap_ref, obp_ref, oas_ref, obs_ref, ga_ref, gb_ref, wbg_ref, wba_ref, wo_ref,
                     g2_ref, win_ref, wout_ref, gf_ref, yp_ref, ys_ref, wbg_s, wba_s, wo_s, win_s, wout_s,
                     *, n_prompt):
    step = pl.program_id(0)
    _stage(step, wbg_ref, wbg_s, 0)
    _stage(step, wba_ref, wba_s, 0)
    _stage(step, wo_ref, wo_s, 0)
    _stage(step, win_ref, win_s, 1)
    _stage(step, wout_ref, wout_s, 0)

    def tile(oa_ref, ob_ref, y_ref):
        a = jnp.dot(oa_ref[...], wbg_s[...], preferred_element_type=F32)
        b = jnp.dot(ob_ref[...], wba_s[...], preferred_element_type=F32)
        mixed = (jax.nn.sigmoid(ga_ref[...].astype(F32)) * a
                 + jax.nn.sigmoid(gb_ref[...].astype(F32)) * b)
        h = h_ref[...] + jnp.dot(mixed.astype(BF16), wo_s[...], preferred_element_type=F32)
        xn = _rms(h, g2_ref[...]).astype(BF16)
        h = h + 0.5 * _swiglu(xn, win_s, wout_s)
        y_ref[...] = _rms(h, gf_ref[...])

    @pl.when((step >= W_STEPS) & (step < W_STEPS + n_prompt))
    def _():
        tile(oap_ref, obp_ref, yp_ref)

    @pl.when(step >= W_STEPS + n_prompt)
    def _():
        tile(oas_ref, obs_ref, ys_ref)


def _mix_ffn2(h, oa_p, ob_p, oa_s, ob_s, p, wbg, wba, wo, g2, win, wout, gf):
    tiles = h.shape[0] // ROW_TILE
    n_prompt, n_sample = oa_p.shape[0] // ROW_TILE, oa_s.shape[0] // ROW_TILE
    assert tiles == n_prompt + n_sample
    once = dict(pipeline_mode=pl.Buffered(1))
    prompt, sample = _row_tiles(0, n_prompt), _row_tiles(n_prompt, n_sample, **once)
    square = _slab_spec(wbg.shape, 0, SQUARE_STEPS, **once)
    return pl.pallas_call(
        functools.partial(_mix_ffn2_kernel, n_prompt=n_prompt),
        grid=(W_STEPS + tiles,),
        in_specs=[_row_tiles(0, tiles), prompt, prompt, sample, sample,
                  _row_tiles(0, tiles, col=COL_GA // D_MODEL), _row_tiles(0, tiles, col=COL_GB // D_MODEL),
                  square, square, square, _const_spec(g2.shape), _slab_spec(win.shape, 1, FFN_SLABS),
                  _slab_spec(wout.shape, 0, FFN_SLABS), _const_spec(gf.shape)],
        out_specs=[prompt, sample],
        out_shape=[jax.ShapeDtypeStruct((n_prompt * ROW_TILE, D_MODEL), F32),
                   jax.ShapeDtypeStruct((n_sample * ROW_TILE, D_MODEL), F32)],
        scratch_shapes=[pltpu.VMEM(wbg.shape, BF16), pltpu.VMEM(wba.shape, BF16), pltpu.VMEM(wo.shape, BF16),
                        pltpu.VMEM(win.shape, BF16), pltpu.VMEM(wout.shape, BF16)],
        compiler_params=_params("arbitrary"),
        name="mix_ffn2",
    )(h, oa_p, ob_p, oa_s, ob_s, p, p, wbg, wba, wo, g2, win, wout, gf)


def _kept_rows(p, row0, batch, t, col):
    keep = min(ATT_PAST, t)
    rows = p[row0:row0 + batch * t, col:col + ATT_WIDTH].reshape(batch, t, ATT_WIDTH)[:, t - keep:]
    return rows.astype(F32).reshape(1, batch, keep, ATT_HEADS, ATT_DH)


def kernel(x_prompt, x_sample, cache_att_k, cache_att_v, state_gla, norm_ffn1, w_ffn1_in, w_ffn1_out,
           norm_mix, w_in, w_gla_gate, b_gla_gate, gla_norm, attn_rel_bias, w_branch_gla, w_branch_att,
           w_out, norm_ffn2, w_ffn2_in, w_ffn2_out, norm_final):
    assert norm_ffn1.shape[0] == 1, "single layer"
    bp, tp, _ = x_prompt.shape
    bs, ts, _ = x_sample.shape
    mp, ms = bp * tp, bs * ts
    assert mp % ROW_TILE == 0 and ms % ROW_TILE == 0
    assert cache_att_k.shape[2] == ATT_PAST and ts <= ATT_TQ

    wg = jnp.concatenate([w_gla_gate[0], jnp.zeros((LANES - GLA_GATE_RANK, GLA_K_WIDTH), F32)],
                         axis=0).astype(BF16)
    rel = _rel_row(attn_rel_bias[0])

    h, n = _ffn1(x_prompt.reshape(mp, D_MODEL), x_sample.reshape(ms, D_MODEL), norm_ffn1, w_ffn1_in[0],
                 w_ffn1_out[0], norm_mix)
    p, v_t, k_prompt_t, v_prompt_t = _proj(n, w_in[0].T, bp, tp)
    to_rows_last = lambda a: jnp.transpose(a, (0, 1, 3, 4, 2))
    to_rows_first = lambda a: jnp.transpose(a, (0, 1, 4, 2, 3))
    gla_w = (wg, b_gla_gate, gla_norm)
    oa_p, s_prompt = _gla(p, *gla_w, jnp.zeros((bp, GLA_HEADS, GLA_DK, GLA_DV), F32), bp, tp, 0)
    oa_s, s_sample = _gla(p, *gla_w, state_gla[0], bs, ts, mp)
    ob_p = _band_prompt(p, v_t, rel, bp, tp)
    ob_s = _band_step(p, to_rows_last(cache_att_k), to_rows_last(cache_att_v), rel, bs, ts, mp)
    yp, ys = _mix_ffn2(h, oa_p, ob_p, oa_s, ob_s, p, w_branch_gla[0], w_branch_att[0], w_out[0], norm_ffn2,
                       w_ffn2_in[0], w_ffn2_out[0], norm_final.reshape(1, D_MODEL))
    return (yp.reshape(bp, tp, D_MODEL), ys.reshape(bs, ts, D_MODEL), to_rows_first(k_prompt_t),
            to_rows_first(v_prompt_t), s_prompt[None],
            _kept_rows(p, mp, bs, ts, COL_KB), _kept_rows(p, mp, bs, ts, COL_VB), s_sample[None])
```

```python
import functools

import jax
import jax.numpy as jnp
import numpy as np
from jax import lax
from jax.experimental import pallas as pl
from jax.experimental.pallas import tpu as pltpu

F32 = jnp.float32
BF16 = jnp.bfloat16

D_MODEL = 1024
CHUNK = 64
GLA_HEADS = 4
GLA_DK = 128
GLA_DV = 256
GLA_K_WIDTH = GLA_HEADS * GLA_DK
GLA_V_WIDTH = GLA_HEADS * GLA_DV
GLA_GATE_RANK = 16
GLA_TAU = 16.0
ATT_HEADS = 16
ATT_DH = 64
ATT_WIDTH = ATT_HEADS * ATT_DH
ATT_PAST = 8 * CHUNK
REL_CLIP = 128
D_FF = 2816
EPS = 1e-6
NEG_INF = -1e30
LOG2E = 1.4426950408889634

LANES = 128
VMEM_LIMIT = 60 * 1024 * 1024

COL_QA = 0
COL_KA = COL_QA + GLA_K_WIDTH
COL_VA = COL_KA + GLA_K_WIDTH
COL_RA = COL_VA + GLA_V_WIDTH
COL_QB = COL_RA + GLA_V_WIDTH
COL_KB = COL_QB + ATT_WIDTH
COL_VB = COL_KB + ATT_WIDTH
COL_GA = COL_VB + ATT_WIDTH
COL_GB = COL_GA + D_MODEL
COL_FA = COL_GB + D_MODEL
PROJ_COLS = COL_FA + LANES


def _const_spec(shape):
    nd = len(shape)
    return pl.BlockSpec(shape, lambda *_: (0,) * nd, pipeline_mode=pl.Buffered(1))


def _params(*sem):
    return pltpu.CompilerParams(dimension_semantics=sem, vmem_limit_bytes=VMEM_LIMIT)


def _rms(x, g):
    return x * lax.rsqrt(jnp.mean(x * x, axis=-1, keepdims=True) + EPS) * g


def _swiglu(xn, win_ref, wout_ref):
    gate = jnp.dot(xn, win_ref[:, :D_FF], preferred_element_type=F32)
    up = jnp.dot(xn, win_ref[:, D_FF:], preferred_element_type=F32)
    act = (gate * jax.nn.sigmoid(gate) * up).astype(BF16)
    return jnp.dot(act, wout_ref[...], preferred_element_type=F32)


W_STEPS = 11
FFN_SLABS = 11


def _slab_spec(shape, axis, steps, **kw):
    block = tuple(d // steps if a == axis else d for a, d in enumerate(shape))
    assert block[axis] * steps == shape[axis] and steps <= W_STEPS
    if axis == 0:
        return pl.BlockSpec(block, lambda i: (jnp.minimum(i, steps - 1), 0), **kw)
    return pl.BlockSpec(block, lambda i: (0, jnp.minimum(i, steps - 1)), **kw)


def _stage(step, src_ref, dst_ref, axis):
    n = src_ref.shape[axis]
    for c in range(dst_ref.shape[axis] // n):
        @pl.when(step == c)
        def _(c=c):
            if axis == 0:
                dst_ref[c * n:(c + 1) * n, :] = src_ref[...].astype(BF16)
            else:
                dst_ref[:, c * n:(c + 1) * n] = src_ref[...].astype(BF16)


ROW_TILE = 512


def _row_tiles(first, count, cols=D_MODEL, col=0, **kw):
    return pl.BlockSpec((ROW_TILE, cols), lambda i: (jnp.clip(i - W_STEPS - first, 0, count - 1), col), **kw)


def _row_halves():
    half = ROW_TILE // 2
    return slice(0, half), slice(half, ROW_TILE)


def _ffn1_kernel(xp_ref, xs_ref, g1_ref, win_ref, wout_ref, gmix_ref, h_ref, n_ref, win_s, wout_s, *,
                 n_prompt):
    step = pl.program_id(0)
    _stage(step, win_ref, win_s, 1)
    _stage(step, wout_ref, wout_s, 0)

    def tile(x_ref):
        for rows in _row_halves():
            x = x_ref[rows, :]
            xn = _rms(x, g1_ref[...]).astype(BF16)
            h = x + 0.5 * _swiglu(xn, win_s, wout_s)
            h_ref[rows, :] = h
            n_ref[rows, :] = _rms(h, gmix_ref[...]).astype(BF16)

    @pl.when((step >= W_STEPS) & (step < W_STEPS + n_prompt))
    def _():
        tile(xp_ref)

    @pl.when(step >= W_STEPS + n_prompt)
    def _():
        tile(xs_ref)


def _ffn1(xp, xs, g1, win, wout, gmix):
    n_prompt, n_sample = xp.shape[0] // ROW_TILE, xs.shape[0] // ROW_TILE
    tiles = n_prompt + n_sample
    m = tiles * ROW_TILE
    return pl.pallas_call(
        functools.partial(_ffn1_kernel, n_prompt=n_prompt),
        grid=(W_STEPS + tiles,),
        in_specs=[_row_tiles(0, n_prompt), _row_tiles(n_prompt, n_sample, pipeline_mode=pl.Buffered(1)),
                  _const_spec(g1.shape), _slab_spec(win.shape, 1, FFN_SLABS),
                  _slab_spec(wout.shape, 0, FFN_SLABS), _const_spec(gmix.shape)],
        out_specs=[_row_tiles(0, tiles), _row_tiles(0, tiles)],
        out_shape=[jax.ShapeDtypeStruct((m, D_MODEL), F32), jax.ShapeDtypeStruct((m, D_MODEL), BF16)],
        scratch_shapes=[pltpu.VMEM(win.shape, BF16), pltpu.VMEM(wout.shape, BF16)],
        compiler_params=_params("arbitrary"),
        name="ffn1",
    )(xp, xs, g1, win, wout, gmix)


PROJ_STEP = 1024


W_IN_SLAB = 1024
W_IN_GATE0 = COL_QB


def _stage_w_in(step, src_ref, dst_ref, n_cols):
    n = src_ref.shape[0]
    g0, g1 = W_IN_GATE0, W_IN_GATE0 + GLA_GATE_RANK
    moves = [(0, g0, 0), (g0, g1, COL_FA), (g1, n_cols, g0)]
    for c in range(pl.cdiv(n_cols, n)):
        @pl.when(step == c)
        def _(c=c):
            lo, hi = c * n, min((c + 1) * n, n_cols)
            for first, end, new in moves:
                a, b = max(lo, first), min(hi, end)
                if a < b:
                    dst_ref[new + a - first:new + b - first, :] = src_ref[a - lo:b - lo, :].astype(BF16)
            if c == 0:
                pad = dst_ref.shape[0] - COL_FA - GLA_GATE_RANK
                dst_ref[COL_FA + GLA_GATE_RANK:, :] = jnp.zeros((pad, dst_ref.shape[1]), BF16)


def _proj_kernel(n_ref, w_ref, o_ref, vt_ref, kp_ref, vp_ref, w_s, *, n_prompt, seq_tiles, n_cols):
    step = pl.program_id(0)
    _stage_w_in(step, w_ref, w_s, n_cols)

    @pl.when(step >= W_STEPS)
    def _():
        tile = step - W_STEPS
        n = n_ref[...]
        kept = {}
        for c0 in range(0, PROJ_COLS, PROJ_STEP):
            c1 = min(c0 + PROJ_STEP, PROJ_COLS)
            acc = lax.dot_general(n, w_s[c0:c1, :], (((1,), (1,)), ((), ())), preferred_element_type=F32)
            o_ref[:, c0:c1] = acc.astype(BF16)
            if c0 in (COL_KB, COL_VB):
                kept[c0] = acc
        v_t = kept[COL_VB].T
        vt_ref[...] = v_t.astype(BF16)

        @pl.when((tile < n_prompt) & (tile % seq_tiles == seq_tiles - 1))
        def _():
            kp_ref[...] = kept[COL_KB].T.reshape(kp_ref.shape)
            vp_ref[...] = v_t.reshape(vp_ref.shape)


def _proj(n, w_t, bp, tp):
    tiles = n.shape[0] // ROW_TILE
    n_prompt, seq_tiles = bp * tp // ROW_TILE, tp // ROW_TILE
    n_cols = w_t.shape[0]
    assert PROJ_STEP == ATT_WIDTH and min(ATT_PAST, tp) == ROW_TILE and pl.cdiv(n_cols, W_IN_SLAB) <= W_STEPS
    last_slab = pl.cdiv(n_cols, W_IN_SLAB) - 1
    kept = pl.BlockSpec((None, None, ATT_HEADS, ATT_DH, ROW_TILE),
                        lambda i: (0, jnp.clip((i - W_STEPS) // seq_tiles, 0, bp - 1), 0, 0, 0))
    return pl.pallas_call(
        functools.partial(_proj_kernel, n_prompt=n_prompt, seq_tiles=seq_tiles, n_cols=n_cols),
        grid=(W_STEPS + tiles,),
        in_specs=[_row_tiles(0, tiles),
                  pl.BlockSpec((W_IN_SLAB, D_MODEL), lambda i: (jnp.minimum(i, last_slab), 0))],
        out_specs=[_row_tiles(0, tiles, cols=PROJ_COLS),
                   pl.BlockSpec((ATT_WIDTH, ROW_TILE), lambda i: (0, jnp.clip(i - W_STEPS, 0, tiles - 1))),
                   kept, kept],
        out_shape=[jax.ShapeDtypeStruct((tiles * ROW_TILE, PROJ_COLS), BF16),
                   jax.ShapeDtypeStruct((ATT_WIDTH, tiles * ROW_TILE), BF16)]
        + [jax.ShapeDtypeStruct((1, bp, ATT_HEADS, ATT_DH, ROW_TILE), F32)] * 2,
        scratch_shapes=[pltpu.VMEM((PROJ_COLS, D_MODEL), BF16)],
        compiler_params=_params("arbitrary"),
        name="proj",
    )(n, w_t)


GLA_ROWS = 2048


GLA_GROUP = 2


def _gla_head(q, k, v, r, f, wg, bg, gn, s0s):
    cl = CHUNK
    nc = q.shape[0] // cl
    per_seq = nc // len(s0s)
    bmm = functools.partial(jnp.einsum, preferred_element_type=F32)

    z = jnp.dot(f, wg, preferred_element_type=F32) + bg
    log_a = (jnp.minimum(z, 0.0) - jnp.log(1.0 + jnp.exp(-jnp.abs(z)))) * (LOG2E / GLA_TAU)
    log_a = log_a.reshape(nc, cl, GLA_DK)

    ri = lax.broadcasted_iota(jnp.int32, (cl, cl), 0)
    ci = lax.broadcasted_iota(jnp.int32, (cl, cl), 1)
    causal = ri >= ci
    tri = jnp.broadcast_to(causal.astype(BF16), (nc, cl, cl))
    hi = log_a.astype(BF16)
    lo = (log_a - hi.astype(F32)).astype(BF16)
    b = bmm('nij,njd->nid', tri, hi) + bmm('nij,njd->nid', tri, lo)
    b_last = b[:, cl - 1:cl, :]

    q = q.astype(F32).reshape(nc, cl, GLA_DK) * (GLA_DK ** -0.5)
    k = k.astype(F32).reshape(nc, cl, GLA_DK)
    v = v.reshape(nc, cl, GLA_DV)
    q_dec = (q * jnp.exp2(b)).astype(BF16)
    k_inv = (k * jnp.exp2(-b)).astype(BF16)
    k_end = (k * jnp.exp2(b_last - b)).astype(BF16)

    scores = jnp.where(causal, bmm('nid,njd->nij', q_dec, k_inv), 0.0)
    o_intra = bmm('nij,njv->niv', scores.astype(BF16), v)
    incr_t = bmm('njv,njd->nvd', v, k_end)
    decay = jnp.exp2(b_last)

    before, states = [], []
    for i, s0 in enumerate(s0s):
        s = s0.T
        for c in range(i * per_seq, (i + 1) * per_seq):
            before.append(s.astype(BF16))
            s = s * decay[c] + incr_t[c]
        states.append(s.T)
    s_before = jnp.stack(before)
    o = o_intra + bmm('nid,nvd->niv', q_dec, s_before)

    o = o * lax.rsqrt(jnp.mean(o * o, axis=-1, keepdims=True) + EPS) * gn
    r = r.astype(F32)
    return o.reshape(nc * cl, GLA_DV) * (r * jax.nn.sigmoid(r)), states


def _gla_kernel(q_ref, k_ref, v_ref, r_ref, f_ref, wg_ref, bg_ref, gn_ref, s0_ref, o_ref, s_ref):
    n_seq = s0_ref.shape[0]
    for j in range(GLA_GROUP):
        dk = slice(j * GLA_DK, (j + 1) * GLA_DK)
        dv = slice(j * GLA_DV, (j + 1) * GLA_DV)
        o, states = _gla_head(q_ref[:, dk], k_ref[:, dk], v_ref[:, dv], r_ref[:, dv], f_ref[...],
                              wg_ref[:, dk], bg_ref[:, dk], gn_ref[:, dv],
                              [s0_ref[i, j] for i in range(n_seq)])
        o_ref[:, dv] = o.astype(BF16)
        for i, s in enumerate(states):
            s_ref[i, j] = s


def _gla(p, wg, bg, gn, s0, batch, t, row0):
    wk, wv = GLA_GROUP * GLA_DK, GLA_GROUP * GLA_DV
    cq, ck, cv, cr = COL_QA // wk, COL_KA // wk, COL_VA // wv, COL_RA // wv
    n_seq = min(batch, max(1, GLA_ROWS // t))
    rows = n_seq * t
    assert row0 % rows == 0 and batch % n_seq == 0
    b0 = row0 // rows
    state = pl.BlockSpec((n_seq, GLA_GROUP, GLA_DK, GLA_DV), lambda b, g: (b, g, 0, 0))
    return pl.pallas_call(
        _gla_kernel,
        grid=(batch // n_seq, GLA_HEADS // GLA_GROUP),
        in_specs=[
            pl.BlockSpec((rows, wk), lambda b, g: (b0 + b, cq + g)),
            pl.BlockSpec((rows, wk), lambda b, g: (b0 + b, ck + g)),
            pl.BlockSpec((rows, wv), lambda b, g: (b0 + b, cv + g)),
            pl.BlockSpec((rows, wv), lambda b, g: (b0 + b, cr + g)),
            pl.BlockSpec((rows, LANES), lambda b, g: (b0 + b, COL_FA // LANES)),
            pl.BlockSpec((LANES, wk), lambda b, g: (0, g)),
            pl.BlockSpec((1, wk), lambda b, g: (0, g)),
            pl.BlockSpec((1, wv), lambda b, g: (0, g)),
            state,
        ],
        out_specs=[pl.BlockSpec((rows, wv), lambda b, g: (b, g)), state],
        out_shape=[jax.ShapeDtypeStruct((batch * t, GLA_V_WIDTH), BF16),
                   jax.ShapeDtypeStruct((batch, GLA_HEADS, GLA_DK, GLA_DV), F32)],
        compiler_params=_params("parallel", "parallel"),
        name="gla",
    )(p, p, p, p, p, wg, bg, gn, s0)


ATT_GROUP = 4
ATT_LANES = ATT_GROUP * ATT_DH
ATT_TQ = 256
REL_RING = 1024


def _rel_row(table):
    u = np.arange(REL_RING)
    u = np.where(u < ATT_PAST + ATT_TQ, u, u - REL_RING)
    idx = np.clip(ATT_PAST - u, -REL_CLIP, REL_CLIP) + REL_CLIP
    return table[:, idx].astype(F32)[:, None, :]


def _build_bias(rel_ref, bias_ref):
    _, tq, nk = bias_ref.shape
    qc = lax.broadcasted_iota(jnp.int32, (tq, nk), 0) // CHUNK
    kc = lax.broadcasted_iota(jnp.int32, (tq, nk), 1) // CHUNK
    visible = (kc >= qc) & (kc <= qc + ATT_PAST // CHUNK)
    for hh in range(bias_ref.shape[0]):
        row = jnp.broadcast_to(rel_ref[hh], (tq, REL_RING))
        toeplitz = pltpu.roll(row, 0, 1, stride=1, stride_axis=0)
        bias_ref[hh] = jnp.where(visible, toeplitz[:, :nk] * LOG2E, NEG_INF)


def _head_lanes(hh):
    lane = lax.broadcasted_iota(jnp.int32, (1, ATT_LANES), 1)
    return (lane >= hh * ATT_DH) & (lane < (hh + 1) * ATT_DH)


SUM_ROWS = 16


def _head_values_t(v_t):
    ones = jnp.ones((SUM_ROWS, v_t.shape[1]), v_t.dtype)
    return [jnp.concatenate([v_t[hh * ATT_DH:(hh + 1) * ATT_DH, :], ones], axis=0) for hh in range(ATT_GROUP)]


def _attend(q, k, vhs_t, bias_ref, boff):
    nk = k.shape[0]
    nt = (((1,), (1,)), ((), ()))
    q = q.astype(F32) * (ATT_DH ** -0.5 * LOG2E)
    outs_t = []
    for hh in range(ATT_GROUP):
        qh = jnp.where(_head_lanes(hh), q, 0.0).astype(BF16)
        s = lax.dot_general(qh, k, nt, preferred_element_type=F32) + bias_ref[hh, :, boff:boff + nk]
        e = jnp.exp2(s - jnp.max(s, axis=-1, keepdims=True)).astype(BF16)
        pv_t = lax.dot_general(vhs_t[hh], e, nt, preferred_element_type=F32)
        outs_t.append(pv_t[:ATT_DH] / pv_t[ATT_DH:ATT_DH + 1])
    return jnp.concatenate(outs_t, axis=0).T.astype(BF16)


def _band_prompt_kernel(rel_ref, q_ref, k_ref, vt_ref, o_ref, bias_ref):
    _, tq, nk = bias_ref.shape
    n_blocks = q_ref.shape[0] // tq
    n_past = ATT_PAST // tq

    @pl.when(pl.program_id(1) == 0)
    def _():
        _build_bias(rel_ref, bias_ref)

    vhs_t = _head_values_t(vt_ref[...])
    for tt in range(n_blocks):
        rows = slice(tt * tq, (tt + 1) * tq)
        keys = slice(max(tt - n_past, 0) * tq, (tt + 1) * tq)
        o_ref[rows, :] = _attend(q_ref[rows, :], k_ref[keys, :], [vh[:, keys] for vh in vhs_t], bias_ref,
                                 max(n_past - tt, 0) * tq)


def _band_prompt(p, v_t, rel, batch, t):
    w = ATT_LANES
    cq, ck = COL_QB // w, COL_KB // w
    return pl.pallas_call(
        _band_prompt_kernel,
        grid=(ATT_HEADS // ATT_GROUP, batch),
        in_specs=[
            pl.BlockSpec((ATT_GROUP, 1, REL_RING), lambda g, b: (g, 0, 0)),
            pl.BlockSpec((t, w), lambda g, b: (b, cq + g)),
            pl.BlockSpec((t, w), lambda g, b: (b, ck + g)),
            pl.BlockSpec((w, t), lambda g, b: (g, b)),
        ],
        out_specs=pl.BlockSpec((t, w), lambda g, b: (b, g)),
        out_shape=jax.ShapeDtypeStruct((batch * t, ATT_WIDTH), BF16),
        scratch_shapes=[pltpu.VMEM((ATT_GROUP, ATT_TQ, ATT_PAST + ATT_TQ), F32)],
        compiler_params=_params("arbitrary", "arbitrary"),
        name="band_prompt",
    )(rel, p, p, v_t)


def _attend_cached(q, kc_t, vc_t, k_new, v_new, bias):
    tq, c = q.shape[0], kc_t.shape[1]
    nt = (((1,), (1,)), ((), ()))
    q = q.astype(F32) * (ATT_DH ** -0.5 * LOG2E)
    qs = jnp.concatenate([jnp.where(_head_lanes(hh), q, 0.0) for hh in range(ATT_GROUP)], axis=0).astype(BF16)
    s = jnp.concatenate([jnp.dot(qs, kc_t, preferred_element_type=F32),
                         lax.dot_general(qs, k_new, nt, preferred_element_type=F32)], axis=1) + bias
    e = jnp.exp2(s - jnp.max(s, axis=-1, keepdims=True))
    eb = e.astype(BF16)
    pv = (lax.dot_general(eb[:, :c], vc_t, nt, preferred_element_type=F32)
          + jnp.dot(eb[:, c:], v_new, preferred_element_type=F32)) / jnp.sum(e, axis=-1, keepdims=True)
    out = pv[:tq]
    for hh in range(1, ATT_GROUP):
        out = jnp.where(_head_lanes(hh), pv[hh * tq:(hh + 1) * tq], out)
    return out.astype(BF16)


def _band_step_kernel(rel_ref, q_ref, k_ref, v_ref, ck_ref, cv_ref, o_ref, bias_ref):
    @pl.when(pl.program_id(0) == 0)
    def _():
        _build_bias(rel_ref, bias_ref)

    t, nk = bias_ref.shape[1:]
    c = ck_ref.shape[2]
    kc_t = ck_ref[...].reshape(ATT_WIDTH, c).astype(BF16)
    vc_t = cv_ref[...].reshape(ATT_WIDTH, c).astype(BF16)
    for g in range(ATT_HEADS // ATT_GROUP):
        lanes = slice(g * ATT_LANES, (g + 1) * ATT_LANES)
        bias = bias_ref[g * ATT_GROUP:(g + 1) * ATT_GROUP].reshape(ATT_GROUP * t, nk)
        o_ref[:, lanes] = _attend_cached(q_ref[:, lanes], kc_t[lanes, :], vc_t[lanes, :], k_ref[:, lanes],
                                         v_ref[:, lanes], bias)


def _band_step(p, cache_k_t, cache_v_t, rel, batch, t, row0):
    c = cache_k_t.shape[4]
    assert row0 % t == 0
    r0 = row0 // t
    new = lambda col: pl.BlockSpec((t, ATT_WIDTH), lambda b: (r0 + b, col // ATT_WIDTH))
    cache = pl.BlockSpec((None, None, ATT_HEADS, ATT_DH, c), lambda b: (0, b, 0, 0, 0))
    return pl.pallas_call(
        _band_step_kernel,
        grid=(batch,),
        in_specs=[_const_spec(rel.shape), new(COL_QB), new(COL_KB), new(COL_VB), cache, cache],
        out_specs=pl.BlockSpec((t, ATT_WIDTH), lambda b: (b, 0)),
        out_shape=jax.ShapeDtypeStruct((batch * t, ATT_WIDTH), BF16),
        scratch_shapes=[pltpu.VMEM((ATT_HEADS, t, c + t), F32)],
        compiler_params=_params("arbitrary"),
        name="band_step",
    )(rel, p, p, p, cache_k_t, cache_v_t)


SQUARE_STEPS = 8


def _mix_ffn2_kernel(h_ref, oap_ref, obp_ref, oas_ref, obs_ref, ga_ref, gb_ref, wbg_ref, wba_ref, wo_ref,
                     g2_ref, win_ref, wout_ref, gf_ref, yp_ref, ys_ref, wbg_s, wba_s, wo_s, win_s, wout_s,
                     *, n_prompt):
    step = pl.program_id(0)
    _stage(step, wbg_ref, wbg_s, 0)
    _stage(step, wba_ref, wba_s, 0)
    _stage(step, wo_ref, wo_s, 0)
    _stage(step, win_ref, win_s, 1)
    _stage(step, wout_ref, wout_s, 0)

    def tile(oa_ref, ob_ref, y_ref):
        for rows in _row_halves():
            a = jnp.dot(oa_ref[rows, :], wbg_s[...], preferred_element_type=F32)
            b = jnp.dot(ob_ref[rows, :], wba_s[...], preferred_element_type=F32)
            mixed = (jax.nn.sigmoid(ga_ref[rows, :].astype(F32)) * a
                     + jax.nn.sigmoid(gb_ref[rows, :].astype(F32)) * b)
            h = h_ref[rows, :] + jnp.dot(mixed.astype(BF16), wo_s[...], preferred_element_type=F32)
            xn = _rms(h, g2_ref[...]).astype(BF16)
            h = h + 0.5 * _swiglu(xn, win_s, wout_s)
            y_ref[rows, :] = _rms(h, gf_ref[...])

    @pl.when((step >= W_STEPS) & (step < W_STEPS + n_prompt))
    def _():
        tile(oap_ref, obp_ref, yp_ref)

    @pl.when(step >= W_STEPS + n_prompt)
    def _():
        tile(oas_ref, obs_ref, ys_ref)


def _mix_ffn2(h, oa_p, ob_p, oa_s, ob_s, p, wbg, wba, wo, g2, win, wout, gf):
    tiles = h.shape[0] // ROW_TILE
    n_prompt, n_sample = oa_p.shape[0] // ROW_TILE, oa_s.shape[0] // ROW_TILE
    assert tiles == n_prompt + n_sample
    once = dict(pipeline_mode=pl.Buffered(1))
    prompt, sample = _row_tiles(0, n_prompt), _row_tiles(n_prompt, n_sample, **once)
    square = _slab_spec(wbg.shape, 0, SQUARE_STEPS, **once)
    return pl.pallas_call(
        functools.partial(_mix_ffn2_kernel, n_prompt=n_prompt),
        grid=(W_STEPS + tiles,),
        in_specs=[_row_tiles(0, tiles), prompt, prompt, sample, sample,
                  _row_tiles(0, tiles, col=COL_GA // D_MODEL), _row_tiles(0, tiles, col=COL_GB // D_MODEL),
                  square, square, square, _const_spec(g2.shape), _slab_spec(win.shape, 1, FFN_SLABS),
                  _slab_spec(wout.shape, 0, FFN_SLABS), _const_spec(gf.shape)],
        out_specs=[prompt, sample],
        out_shape=[jax.ShapeDtypeStruct((n_prompt * ROW_TILE, D_MODEL), F32),
                   jax.ShapeDtypeStruct((n_sample * ROW_TILE, D_MODEL), F32)],
        scratch_shapes=[pltpu.VMEM(wbg.shape, BF16), pltpu.VMEM(wba.shape, BF16), pltpu.VMEM(wo.shape, BF16),
                        pltpu.VMEM(win.shape, BF16), pltpu.VMEM(wout.shape, BF16)],
        compiler_params=_params("arbitrary"),
        name="mix_ffn2",
    )(h, oa_p, ob_p, oa_s, ob_s, p, p, wbg, wba, wo, g2, win, wout, gf)


def _kept_rows(p, row0, batch, t, col):
    keep = min(ATT_PAST, t)
    rows = p[row0:row0 + batch * t, col:col + ATT_WIDTH].reshape(batch, t, ATT_WIDTH)[:, t - keep:]
    return rows.astype(F32).reshape(1, batch, keep, ATT_HEADS, ATT_DH)


def kernel(x_prompt, x_sample, cache_att_k, cache_att_v, state_gla, norm_ffn1, w_ffn1_in, w_ffn1_out,
           norm_mix, w_in, w_gla_gate, b_gla_gate, gla_norm, attn_rel_bias, w_branch_gla, w_branch_att,
           w_out, norm_ffn2, w_ffn2_in, w_ffn2_out, norm_final):
    assert norm_ffn1.shape[0] == 1, "single layer"
    bp, tp, _ = x_prompt.shape
    bs, ts, _ = x_sample.shape
    mp, ms = bp * tp, bs * ts
    assert mp % ROW_TILE == 0 and ms % ROW_TILE == 0
    assert cache_att_k.shape[2] == ATT_PAST and ts <= ATT_TQ

    wg = jnp.concatenate([w_gla_gate[0], jnp.zeros((LANES - GLA_GATE_RANK, GLA_K_WIDTH), F32)],
                         axis=0).astype(BF16)
    rel = _rel_row(attn_rel_bias[0])

    h, n = _ffn1(x_prompt.reshape(mp, D_MODEL), x_sample.reshape(ms, D_MODEL), norm_ffn1, w_ffn1_in[0],
                 w_ffn1_out[0], norm_mix)
    p, v_t, k_prompt_t, v_prompt_t = _proj(n, w_in[0].T, bp, tp)
    to_rows_last = lambda a: jnp.transpose(a, (0, 1, 3, 4, 2))
    to_rows_first = lambda a: jnp.transpose(a, (0, 1, 4, 2, 3))
    gla_w = (wg, b_gla_gate, gla_norm)
    oa_p, s_prompt = _gla(p, *gla_w, jnp.zeros((bp, GLA_HEADS, GLA_DK, GLA_DV), F32), bp, tp, 0)
    oa_s, s_sample = _gla(p, *gla_w, state_gla[0], bs, ts, mp)
    ob_p = _band_prompt(p, v_t, rel, bp, tp)
    ob_s = _band_step(p, to_rows_last(cache_att_k), to_rows_last(cache_att_v), rel, bs, ts, mp)
    yp, ys = _mix_ffn2(h, oa_p, ob_p, oa_s, ob_s, p, w_branch_gla[0], w_branch_att[0], w_out[0], norm_ffn2,
                       w_ffn2_in[0], w_ffn2_out[0], norm_final.reshape(1, D_MODEL))
    return (yp.reshape(bp, tp, D_MODEL), ys.reshape(bs, ts, D_MODEL), to_rows_first(k_prompt_t),
            to_rows_first(v_prompt_t), s_prompt[None],
            _kept_rows(p, mp, bs, ts, COL_KB), _kept_rows(p, mp, bs, ts, COL_VB), s_sample[None])
```

```python
import functools

import jax
import jax.numpy as jnp
import numpy as np
from jax import lax
from jax.experimental import pallas as pl
from jax.experimental.pallas import tpu as pltpu

F32 = jnp.float32
BF16 = jnp.bfloat16

D_MODEL = 1024
CHUNK = 64
GLA_HEADS = 4
GLA_DK = 128
GLA_DV = 256
GLA_K_WIDTH = GLA_HEADS * GLA_DK
GLA_V_WIDTH = GLA_HEADS * GLA_DV
GLA_GATE_RANK = 16
GLA_TAU = 16.0
ATT_HEADS = 16
ATT_DH = 64
ATT_WIDTH = ATT_HEADS * ATT_DH
ATT_PAST = 8 * CHUNK
REL_CLIP = 128
D_FF = 2816
EPS = 1e-6
NEG_INF = -1e30
LOG2E = 1.4426950408889634

LANES = 128
VMEM_LIMIT = 60 * 1024 * 1024

COL_QA = 0
COL_KA = COL_QA + GLA_K_WIDTH
COL_VA = COL_KA + GLA_K_WIDTH
COL_RA = COL_VA + GLA_V_WIDTH
COL_QB = COL_RA + GLA_V_WIDTH
COL_KB = COL_QB + ATT_WIDTH
COL_VB = COL_KB + ATT_WIDTH
COL_GA = COL_VB + ATT_WIDTH
COL_GB = COL_GA + D_MODEL
COL_FA = COL_GB + D_MODEL
PROJ_COLS = COL_FA + LANES


def _const_spec(shape):
    nd = len(shape)
    return pl.BlockSpec(shape, lambda *_: (0,) * nd, pipeline_mode=pl.Buffered(1))


def _params(*sem):
    return pltpu.CompilerParams(dimension_semantics=sem, vmem_limit_bytes=VMEM_LIMIT)


def _rms(x, g):
    return x * lax.rsqrt(jnp.mean(x * x, axis=-1, keepdims=True) + EPS) * g


def _swiglu(xn, win_ref, wout_ref):
    gate = jnp.dot(xn, win_ref[:, :D_FF], preferred_element_type=F32)
    up = jnp.dot(xn, win_ref[:, D_FF:], preferred_element_type=F32)
    act = (gate * jax.nn.sigmoid(gate) * up).astype(BF16)
    return jnp.dot(act, wout_ref[...], preferred_element_type=F32)


W_STEPS = 11
FFN_SLABS = 11


def _slab_spec(shape, axis, steps, **kw):
    block = tuple(d // steps if a == axis else d for a, d in enumerate(shape))
    assert block[axis] * steps == shape[axis] and steps <= W_STEPS
    if axis == 0:
        return pl.BlockSpec(block, lambda i: (jnp.minimum(i, steps - 1), 0), **kw)
    return pl.BlockSpec(block, lambda i: (0, jnp.minimum(i, steps - 1)), **kw)


def _stage(step, src_ref, dst_ref, axis):
    n = src_ref.shape[axis]
    for c in range(dst_ref.shape[axis] // n):
        @pl.when(step == c)
        def _(c=c):
            if axis == 0:
                dst_ref[c * n:(c + 1) * n, :] = src_ref[...].astype(BF16)
            else:
                dst_ref[:, c * n:(c + 1) * n] = src_ref[...].astype(BF16)


ROW_TILE = 512


def _row_tiles(first, count, cols=D_MODEL, col=0, **kw):
    return pl.BlockSpec((ROW_TILE, cols), lambda i: (jnp.clip(i - W_STEPS - first, 0, count - 1), col), **kw)


def _row_halves():
    half = ROW_TILE // 2
    return slice(0, half), slice(half, ROW_TILE)


def _ffn1_kernel(xp_ref, xs_ref, g1_ref, win_ref, wout_ref, gmix_ref, h_ref, n_ref, win_s, wout_s, *,
                 n_prompt):
    step = pl.program_id(0)
    _stage(step, win_ref, win_s, 1)
    _stage(step, wout_ref, wout_s, 0)

    def tile(x_ref):
        for rows in _row_halves():
            x = x_ref[rows, :]
            xn = _rms(x, g1_ref[...]).astype(BF16)
            h = x + 0.5 * _swiglu(xn, win_s, wout_s)
            h_ref[rows, :] = h
            n_ref[rows, :] = _rms(h, gmix_ref[...]).astype(BF16)

    @pl.when((step >= W_STEPS) & (step < W_STEPS + n_prompt))
    def _():
        tile(xp_ref)

    @pl.when(step >= W_STEPS + n_prompt)
    def _():
        tile(xs_ref)


def _ffn1(xp, xs, g1, win, wout, gmix):
    n_prompt, n_sample = xp.shape[0] // ROW_TILE, xs.shape[0] // ROW_TILE
    tiles = n_prompt + n_sample
    m = tiles * ROW_TILE
    return pl.pallas_call(
        functools.partial(_ffn1_kernel, n_prompt=n_prompt),
        grid=(W_STEPS + tiles,),
        in_specs=[_row_tiles(0, n_prompt), _row_tiles(n_prompt, n_sample, pipeline_mode=pl.Buffered(1)),
                  _const_spec(g1.shape), _slab_spec(win.shape, 1, FFN_SLABS),
                  _slab_spec(wout.shape, 0, FFN_SLABS), _const_spec(gmix.shape)],
        out_specs=[_row_tiles(0, tiles), _row_tiles(0, tiles)],
        out_shape=[jax.ShapeDtypeStruct((m, D_MODEL), F32), jax.ShapeDtypeStruct((m, D_MODEL), BF16)],
        scratch_shapes=[pltpu.VMEM(win.shape, BF16), pltpu.VMEM(wout.shape, BF16)],
        compiler_params=_params("arbitrary"),
        name="ffn1",
    )(xp, xs, g1, win, wout, gmix)


PROJ_STEP = 1024


W_IN_SLAB = 1024
W_IN_GATE0 = COL_QB


def _stage_w_in(step, src_ref, dst_ref, n_cols):
    n = src_ref.shape[0]
    g0, g1 = W_IN_GATE0, W_IN_GATE0 + GLA_GATE_RANK
    moves = [(0, g0, 0), (g0, g1, COL_FA), (g1, n_cols, g0)]
    for c in range(pl.cdiv(n_cols, n)):
        @pl.when(step == c)
        def _(c=c):
            lo, hi = c * n, min((c + 1) * n, n_cols)
            for first, end, new in moves:
                a, b = max(lo, first), min(hi, end)
                if a < b:
                    dst_ref[new + a - first:new + b - first, :] = src_ref[a - lo:b - lo, :].astype(BF16)
            if c == 0:
                pad = dst_ref.shape[0] - COL_FA - GLA_GATE_RANK
                dst_ref[COL_FA + GLA_GATE_RANK:, :] = jnp.zeros((pad, dst_ref.shape[1]), BF16)


def _proj_kernel(n_ref, w_ref, o_ref, vt_ref, kp_ref, vp_ref, w_s, *, n_prompt, seq_tiles, n_cols):
    step = pl.program_id(0)
    _stage_w_in(step, w_ref, w_s, n_cols)

    @pl.when(step >= W_STEPS)
    def _():
        tile = step - W_STEPS
        n = n_ref[...]
        kept = {}
        for c0 in range(0, PROJ_COLS, PROJ_STEP):
            c1 = min(c0 + PROJ_STEP, PROJ_COLS)
            acc = lax.dot_general(n, w_s[c0:c1, :], (((1,), (1,)), ((), ())), preferred_element_type=F32)
            o_ref[:, c0:c1] = acc.astype(BF16)
            if c0 in (COL_KB, COL_VB):
                kept[c0] = acc
        v_t = kept[COL_VB].T
        vt_ref[...] = v_t.astype(BF16)

        @pl.when((tile < n_prompt) & (tile % seq_tiles == seq_tiles - 1))
        def _():
            kp_ref[...] = kept[COL_KB].T.reshape(kp_ref.shape)
            vp_ref[...] = v_t.reshape(vp_ref.shape)


def _proj(n, w_t, bp, tp):
    tiles = n.shape[0] // ROW_TILE
    n_prompt, seq_tiles = bp * tp // ROW_TILE, tp // ROW_TILE
    n_cols = w_t.shape[0]
    assert PROJ_STEP == ATT_WIDTH and min(ATT_PAST, tp) == ROW_TILE and pl.cdiv(n_cols, W_IN_SLAB) <= W_STEPS
    last_slab = pl.cdiv(n_cols, W_IN_SLAB) - 1
    kept = pl.BlockSpec((None, None, ATT_HEADS, ATT_DH, ROW_TILE),
                        lambda i: (0, jnp.clip((i - W_STEPS) // seq_tiles, 0, bp - 1), 0, 0, 0))
    return pl.pallas_call(
        functools.partial(_proj_kernel, n_prompt=n_prompt, seq_tiles=seq_tiles, n_cols=n_cols),
        grid=(W_STEPS + tiles,),
        in_specs=[_row_tiles(0, tiles),
                  pl.BlockSpec((W_IN_SLAB, D_MODEL), lambda i: (jnp.minimum(i, last_slab), 0))],
        out_specs=[_row_tiles(0, tiles, cols=PROJ_COLS),
                   pl.BlockSpec((ATT_WIDTH, ROW_TILE), lambda i: (0, jnp.clip(i - W_STEPS, 0, tiles - 1))),
                   kept, kept],
        out_shape=[jax.ShapeDtypeStruct((tiles * ROW_TILE, PROJ_COLS), BF16),
                   jax.ShapeDtypeStruct((ATT_WIDTH, tiles * ROW_TILE), BF16)]
        + [jax.ShapeDtypeStruct((1, bp, ATT_HEADS, ATT_DH, ROW_TILE), F32)] * 2,
        scratch_shapes=[pltpu.VMEM((PROJ_COLS, D_MODEL), BF16)],
        compiler_params=_params("arbitrary"),
        name="proj",
    )(n, w_t)


GLA_ROWS = 2048


GLA_GROUP = 2


def _gla_head(q, k, v, r, f, wg, bg, gn, s0s):
    cl = CHUNK
    nc = q.shape[0] // cl
    per_seq = nc // len(s0s)
    bmm = functools.partial(jnp.einsum, preferred_element_type=F32)

    z = jnp.dot(f, wg, preferred_element_type=F32) + bg
    log_a = (jnp.minimum(z, 0.0) - jnp.log(1.0 + jnp.exp(-jnp.abs(z)))) * (LOG2E / GLA_TAU)
    log_a = log_a.reshape(nc, cl, GLA_DK)

    ri = lax.broadcasted_iota(jnp.int32, (cl, cl), 0)
    ci = lax.broadcasted_iota(jnp.int32, (cl, cl), 1)
    causal = ri >= ci
    tri = jnp.broadcast_to(causal.astype(BF16), (nc, cl, cl))
    hi = log_a.astype(BF16)
    lo = (log_a - hi.astype(F32)).astype(BF16)
    b = bmm('nij,njd->nid', tri, hi) + bmm('nij,njd->nid', tri, lo)
    b_last = b[:, cl - 1:cl, :]

    q = q.astype(F32).reshape(nc, cl, GLA_DK) * (GLA_DK ** -0.5)
    k = k.astype(F32).reshape(nc, cl, GLA_DK)
    v = v.reshape(nc, cl, GLA_DV)
    q_dec = (q * jnp.exp2(b)).astype(BF16)
    k_inv = (k * jnp.exp2(-b)).astype(BF16)
    k_end = (k * jnp.exp2(b_last - b)).astype(BF16)

    scores = jnp.where(causal, bmm('nid,njd->nij', q_dec, k_inv), 0.0)
    o_intra = bmm('nij,njv->niv', scores.astype(BF16), v)
    incr_t = bmm('njv,njd->nvd', v, k_end)
    decay = jnp.exp2(b_last)

    before, states = [], []
    for i, s0 in enumerate(s0s):
        s = s0.T
        for c in range(i * per_seq, (i + 1) * per_seq):
            before.append(s.astype(BF16))
            s = s * decay[c] + incr_t[c]
        states.append(s.T)
    s_before = jnp.stack(before)
    o = o_intra + bmm('nid,nvd->niv', q_dec, s_before)

    o = o * lax.rsqrt(jnp.mean(o * o, axis=-1, keepdims=True) + EPS) * gn
    r = r.astype(F32)
    return o.reshape(nc * cl, GLA_DV) * (r * jax.nn.sigmoid(r)), states


def _gla_kernel(q_ref, k_ref, v_ref, r_ref, f_ref, wg_ref, bg_ref, gn_ref, s0_ref, o_ref, s_ref):
    n_seq = s0_ref.shape[0]
    for j in range(GLA_GROUP):
        dk = slice(j * GLA_DK, (j + 1) * GLA_DK)
        dv = slice(j * GLA_DV, (j + 1) * GLA_DV)
        o, states = _gla_head(q_ref[:, dk], k_ref[:, dk], v_ref[:, dv], r_ref[:, dv], f_ref[...],
                              wg_ref[:, dk], bg_ref[:, dk], gn_ref[:, dv],
                              [s0_ref[i, j] for i in range(n_seq)])
        o_ref[:, dv] = o.astype(BF16)
        for i, s in enumerate(states):
            s_ref[i, j] = s


def _gla(p, wg, bg, gn, s0, batch, t, row0):
    wk, wv = GLA_GROUP * GLA_DK, GLA_GROUP * GLA_DV
    cq, ck, cv, cr = COL_QA // wk, COL_KA // wk, COL_VA // wv, COL_RA // wv
    n_seq = min(batch, max(1, GLA_ROWS // t))
    rows = n_seq * t
    assert row0 % rows == 0 and batch % n_seq == 0
    b0 = row0 // rows
    state = pl.BlockSpec((n_seq, GLA_GROUP, GLA_DK, GLA_DV), lambda b, g: (b, g, 0, 0))
    return pl.pallas_call(
        _gla_kernel,
        grid=(batch // n_seq, GLA_HEADS // GLA_GROUP),
        in_specs=[
            pl.BlockSpec((rows, wk), lambda b, g: (b0 + b, cq + g)),
            pl.BlockSpec((rows, wk), lambda b, g: (b0 + b, ck + g)),
            pl.BlockSpec((rows, wv), lambda b, g: (b0 + b, cv + g)),
            pl.BlockSpec((rows, wv), lambda b, g: (b0 + b, cr + g)),
            pl.BlockSpec((rows, LANES), lambda b, g: (b0 + b, COL_FA // LANES)),
            pl.BlockSpec((LANES, wk), lambda b, g: (0, g)),
            pl.BlockSpec((1, wk), lambda b, g: (0, g)),
            pl.BlockSpec((1, wv), lambda b, g: (0, g)),
            state,
        ],
        out_specs=[pl.BlockSpec((rows, wv), lambda b, g: (b, g)), state],
        out_shape=[jax.ShapeDtypeStruct((batch * t, GLA_V_WIDTH), BF16),
                   jax.ShapeDtypeStruct((batch, GLA_HEADS, GLA_DK, GLA_DV), F32)],
        compiler_params=_params("parallel", "parallel"),
        name="gla",
    )(p, p, p, p, p, wg, bg, gn, s0)


ATT_GROUP = 4
ATT_LANES = ATT_GROUP * ATT_DH
ATT_TQ = 256
REL_RING = 1024


def _rel_row(table):
    u = np.arange(REL_RING)
    u = np.where(u < ATT_PAST + ATT_TQ, u, u - REL_RING)
    idx = np.clip(ATT_PAST - u, -REL_CLIP, REL_CLIP) + REL_CLIP
    return table[:, idx].astype(F32)[:, None, :]


def _build_bias(rel_ref, bias_ref):
    _, tq, nk = bias_ref.shape
    qc = lax.broadcasted_iota(jnp.int32, (tq, nk), 0) // CHUNK
    kc = lax.broadcasted_iota(jnp.int32, (tq, nk), 1) // CHUNK
    visible = (kc >= qc) & (kc <= qc + ATT_PAST // CHUNK)
    for hh in range(bias_ref.shape[0]):
        row = jnp.broadcast_to(rel_ref[hh], (tq, REL_RING))
        toeplitz = pltpu.roll(row, 0, 1, stride=1, stride_axis=0)
        bias_ref[hh] = jnp.where(visible, toeplitz[:, :nk] * LOG2E, NEG_INF)


def _head_lanes(hh):
    lane = lax.broadcasted_iota(jnp.int32, (1, ATT_LANES), 1)
    return (lane >= hh * ATT_DH) & (lane < (hh + 1) * ATT_DH)


SUM_ROWS = 16


def _head_values_t(v_t):
    ones = jnp.ones((SUM_ROWS, v_t.shape[1]), v_t.dtype)
    return [jnp.concatenate([v_t[hh * ATT_DH:(hh + 1) * ATT_DH, :], ones], axis=0) for hh in range(ATT_GROUP)]


def _attend(q, k, vhs_t, bias_ref, boff):
    nk = k.shape[0]
    nt = (((1,), (1,)), ((), ()))
    q = q.astype(F32) * (ATT_DH ** -0.5 * LOG2E)
    outs_t = []
    for hh in range(ATT_GROUP):
        qh = jnp.where(_head_lanes(hh), q, 0.0).astype(BF16)
        s = lax.dot_general(qh, k, nt, preferred_element_type=F32) + bias_ref[hh, :, boff:boff + nk]
        e = jnp.exp2(s - jnp.max(s, axis=-1, keepdims=True)).astype(BF16)
        pv_t = lax.dot_general(vhs_t[hh], e, nt, preferred_element_type=F32)
        outs_t.append(pv_t[:ATT_DH] / pv_t[ATT_DH:ATT_DH + 1])
    return jnp.concatenate(outs_t, axis=0).T.astype(BF16)


def _band_prompt_kernel(rel_ref, q_ref, k_ref, vt_ref, o_ref, bias_ref):
    _, tq, nk = bias_ref.shape
    n_blocks = q_ref.shape[0] // tq
    n_past = ATT_PAST // tq

    @pl.when(pl.program_id(1) == 0)
    def _():
        _build_bias(rel_ref, bias_ref)

    vhs_t = _head_values_t(vt_ref[...])
    for tt in range(n_blocks):
        rows = slice(tt * tq, (tt + 1) * tq)
        keys = slice(max(tt - n_past, 0) * tq, (tt + 1) * tq)
        o_ref[rows, :] = _attend(q_ref[rows, :], k_ref[keys, :], [vh[:, keys] for vh in vhs_t], bias_ref,
                                 max(n_past - tt, 0) * tq)


def _band_prompt(p, v_t, rel, batch, t):
    w = ATT_LANES
    cq, ck = COL_QB // w, COL_KB // w
    return pl.pallas_call(
        _band_prompt_kernel,
        grid=(ATT_HEADS // ATT_GROUP, batch),
        in_specs=[
            pl.BlockSpec((ATT_GROUP, 1, REL_RING), lambda g, b: (g, 0, 0)),
            pl.BlockSpec((t, w), lambda g, b: (b, cq + g)),
            pl.BlockSpec((t, w), lambda g, b: (b, ck + g)),
            pl.BlockSpec((w, t), lambda g, b: (g, b)),
        ],
        out_specs=pl.BlockSpec((t, w), lambda g, b: (b, g)),
        out_shape=jax.ShapeDtypeStruct((batch * t, ATT_WIDTH), BF16),
        scratch_shapes=[pltpu.VMEM((ATT_GROUP, ATT_TQ, ATT_PAST + ATT_TQ), F32)],
        compiler_params=_params("arbitrary", "arbitrary"),
        name="band_prompt",
    )(rel, p, p, v_t)


def _attend_cached(q, kc_t, vc_t, k_new, v_new, bias):
    tq, c = q.shape[0], kc_t.shape[1]
    nt = (((1,), (1,)), ((), ()))
    q = q.astype(F32) * (ATT_DH ** -0.5 * LOG2E)
    qs = jnp.concatenate([jnp.where(_head_lanes(hh), q, 0.0) for hh in range(ATT_GROUP)], axis=0).astype(BF16)
    s = jnp.concatenate([jnp.dot(qs, kc_t, preferred_element_type=F32),
                         lax.dot_general(qs, k_new, nt, preferred_element_type=F32)], axis=1) + bias
    e = jnp.exp2(s - jnp.max(s, axis=-1, keepdims=True))
    eb = e.astype(BF16)
    pv = (lax.dot_general(eb[:, :c], vc_t, nt, preferred_element_type=F32)
          + jnp.dot(eb[:, c:], v_new, preferred_element_type=F32)) / jnp.sum(e, axis=-1, keepdims=True)
    out = pv[:tq]
    for hh in range(1, ATT_GROUP):
        out = jnp.where(_head_lanes(hh), pv[hh * tq:(hh + 1) * tq], out)
    return out.astype(BF16)


def _band_step_kernel(rel_ref, q_ref, k_ref, v_ref, ck_ref, cv_ref, o_ref, bias_ref):
    @pl.when(pl.program_id(0) == 0)
    def _():
        _build_bias(rel_ref, bias_ref)

    t, nk = bias_ref.shape[1:]
    c = ck_ref.shape[2]
    kc_t = ck_ref[...].reshape(ATT_WIDTH, c).astype(BF16)
    vc_t = cv_ref[...].reshape(ATT_WIDTH, c).astype(BF16)
    for g in range(ATT_HEADS // ATT_GROUP):
        lanes = slice(g * ATT_LANES, (g + 1) * ATT_LANES)
        bias = bias_ref[g * ATT_GROUP:(g + 1) * ATT_GROUP].reshape(ATT_GROUP * t, nk)
        o_ref[:, lanes] = _attend_cached(q_ref[:, lanes], kc_t[lanes, :], vc_t[lanes, :], k_ref[:, lanes],
                                         v_ref[:, lanes], bias)


def _band_step(p, cache_k_t, cache_v_t, rel, batch, t, row0):
    c = cache_k_t.shape[4]
    assert row0 % t == 0
    r0 = row0 // t
    new = lambda col: pl.BlockSpec((t, ATT_WIDTH), lambda b: (r0 + b, col // ATT_WIDTH))
    cache = pl.BlockSpec((None, None, ATT_HEADS, ATT_DH, c), lambda b: (0, b, 0, 0, 0))
    return pl.pallas_call(
        _band_step_kernel,
        grid=(batch,),
        in_specs=[_const_spec(rel.shape), new(COL_QB), new(COL_KB), new(COL_VB), cache, cache],
        out_specs=pl.BlockSpec((t, ATT_WIDTH), lambda b: (b, 0)),
        out_shape=jax.ShapeDtypeStruct((batch * t, ATT_WIDTH), BF16),
        scratch_shapes=[pltpu.VMEM((ATT_HEADS, t, c + t), F32)],
        compiler_params=_params("arbitrary"),
        name="band_step",
    )(rel, p, p, p, cache_k_t, cache_v_t)


SQUARE_STEPS = 8


def _mix_ffn2_kernel(h_ref, oap_ref, obp_ref, oas_ref, obs_ref, ga_ref, gb_ref, wbg_ref, wba_ref, wo_ref,
                     g2_ref, win_ref, wout_ref, gf_ref, yp_ref, ys_ref, wbg_s, wba_s, wo_s, win_s, wout_s,
                     *, n_prompt):
    step = pl.program_id(0)
    _stage(step, wbg_ref, wbg_s, 0)
    _stage(step, wba_ref, wba_s, 0)
    _stage(step, wo_ref, wo_s, 0)
    _stage(step, win_ref, win_s, 1)
    _stage(step, wout_ref, wout_s, 0)

    def tile(oa_ref, ob_ref, y_ref):
        a = jnp.dot(oa_ref[...], wbg_s[...], preferred_element_type=F32)
        b = jnp.dot(ob_ref[...], wba_s[...], preferred_element_type=F32)
        mixed = (jax.nn.sigmoid(ga_ref[...].astype(F32)) * a
                 + jax.nn.sigmoid(gb_ref[...].astype(F32)) * b)
        h = h_ref[...] + jnp.dot(mixed.astype(BF16), wo_s[...], preferred_element_type=F32)
        xn = _rms(h, g2_ref[...]).astype(BF16)
        h = h + 0.5 * _swiglu(xn, win_s, wout_s)
        y_ref[...] = _rms(h, gf_ref[...])

    @pl.when((step >= W_STEPS) & (step < W_STEPS + n_prompt))
    def _():
        tile(oap_ref, obp_ref, yp_ref)

    @pl.when(step >= W_STEPS + n_prompt)
    def _():
        tile(oas_ref, obs_ref, ys_ref)


def _mix_ffn2(h, oa_p, ob_p, oa_s, ob_s, p, wbg, wba, wo, g2, win, wout, gf):
    tiles = h.shape[0] // ROW_TILE
    n_prompt, n_sample = oa_p.shape[0] // ROW_TILE, oa_s.shape[0] // ROW_TILE
    assert tiles == n_prompt + n_sample
    once = dict(pipeline_mode=pl.Buffered(1))
    prompt, sample = _row_tiles(0, n_prompt), _row_tiles(n_prompt, n_sample, **once)
    square = _slab_spec(wbg.shape, 0, SQUARE_STEPS, **once)
    return pl.pallas_call(
        functools.partial(_mix_ffn2_kernel, n_prompt=n_prompt),
        grid=(W_STEPS + tiles,),
        in_specs=[_row_tiles(0, tiles), prompt, prompt, sample, sample,
                  _row_tiles(0, tiles, col=COL_GA // D_MODEL), _row_tiles(0, tiles, col=COL_GB // D_MODEL),
                  square, square, square, _const_spec(g2.shape), _slab_spec(win.shape, 1, FFN_SLABS),
                  _slab_spec(wout.shape, 0, FFN_SLABS), _const_spec(gf.shape)],
        out_specs=[prompt, sample],
        out_shape=[jax.ShapeDtypeStruct((n_prompt * ROW_TILE, D_MODEL), F32),
                   jax.ShapeDtypeStruct((n_sample * ROW_TILE, D_MODEL), F32)],
        scratch_shapes=[pltpu.VMEM(wbg.shape, BF16), pltpu.VMEM(wba.shape, BF16), pltpu.VMEM(wo.shape, BF16),
                        pltpu.VMEM(win.shape, BF16), pltpu.VMEM(wout.shape, BF16)],
        compiler_params=_params("arbitrary"),
        name="mix_ffn2",
    )(h, oa_p, ob_p, oa_s, ob_s, p, p, wbg, wba, wo, g2, win, wout, gf)


def _kept_rows(p, row0, batch, t, col):
    keep = min(ATT_PAST, t)
    rows = p[row0:row0 + batch * t, col:col + ATT_WIDTH].reshape(batch, t, ATT_WIDTH)[:, t - keep:]
    return rows.astype(F32).reshape(1, batch, keep, ATT_HEADS, ATT_DH)


def kernel(x_prompt, x_sample, cache_att_k, cache_att_v, state_gla, norm_ffn1, w_ffn1_in, w_ffn1_out,
           norm_mix, w_in, w_gla_gate, b_gla_gate, gla_norm, attn_rel_bias, w_branch_gla, w_branch_att,
           w_out, norm_ffn2, w_ffn2_in, w_ffn2_out, norm_final):
    assert norm_ffn1.shape[0] == 1, "single layer"
    bp, tp, _ = x_prompt.shape
    bs, ts, _ = x_sample.shape
    mp, ms = bp * tp, bs * ts
    assert mp % ROW_TILE == 0 and ms % ROW_TILE == 0
    assert cache_att_k.shape[2] == ATT_PAST and ts <= ATT_TQ

    wg = jnp.concatenate([w_gla_gate[0], jnp.zeros((LANES - GLA_GATE_RANK, GLA_K_WIDTH), F32)],
                         axis=0).astype(BF16)
    rel = _rel_row(attn_rel_bias[0])

    h, n = _ffn1(x_prompt.reshape(mp, D_MODEL), x_sample.reshape(ms, D_MODEL), norm_ffn1, w_ffn1_in[0],
                 w_ffn1_out[0], norm_mix)
    p, v_t, k_prompt_t, v_prompt_t = _proj(n, w_in[0].T, bp, tp)
    to_rows_last = lambda a: jnp.transpose(a, (0, 1, 3, 4, 2))
    to_rows_first = lambda a: jnp.transpose(a, (0, 1, 4, 2, 3))
    gla_w = (wg, b_gla_gate, gla_norm)
    oa_p, s_prompt = _gla(p, *gla_w, jnp.zeros((bp, GLA_HEADS, GLA_DK, GLA_DV), F32), bp, tp, 0)
    oa_s, s_sample = _gla(p, *gla_w, state_gla[0], bs, ts, mp)
    ob_p = _band_prompt(p, v_t, rel, bp, tp)
    ob_s = _band_step(p, to_rows_last(cache_att_k), to_rows_last(cache_att_v), rel, bs, ts, mp)
    yp, ys = _mix_ffn2(h, oa_p, ob_p, oa_s, ob_s, p, w_branch_gla[0], w_branch_att[0], w_out[0], norm_ffn2,
                       w_ffn2_in[0], w_ffn2_out[0], norm_final.reshape(1, D_MODEL))
    return (yp.reshape(bp, tp, D_MODEL), ys.reshape(bs, ts, D_MODEL), to_rows_first(k_prompt_t),
            to_rows_first(v_prompt_t), s_prompt[None],
            _kept_rows(p, mp, bs, ts, COL_KB), _kept_rows(p, mp, bs, ts, COL_VB), s_sample[None])
```

```python
import functools

import jax
import jax.numpy as jnp
import numpy as np
from jax import lax
from jax.experimental import pallas as pl
from jax.experimental.pallas import tpu as pltpu

F32 = jnp.float32
BF16 = jnp.bfloat16

D_MODEL = 1024
CHUNK = 64
GLA_HEADS = 4
GLA_DK = 128
GLA_DV = 256
GLA_K_WIDTH = GLA_HEADS * GLA_DK
GLA_V_WIDTH = GLA_HEADS * GLA_DV
GLA_GATE_RANK = 16
GLA_TAU = 16.0
ATT_HEADS = 16
ATT_DH = 64
ATT_WIDTH = ATT_HEADS * ATT_DH
ATT_PAST = 8 * CHUNK
REL_CLIP = 128
D_FF = 2816
EPS = 1e-6
NEG_INF = -1e30
LOG2E = 1.4426950408889634

LANES = 128
VMEM_LIMIT = 60 * 1024 * 1024

COL_QA = 0
COL_KA = COL_QA + GLA_K_WIDTH
COL_VA = COL_KA + GLA_K_WIDTH
COL_RA = COL_VA + GLA_V_WIDTH
COL_QB = COL_RA + GLA_V_WIDTH
COL_KB = COL_QB + ATT_WIDTH
COL_VB = COL_KB + ATT_WIDTH
COL_GA = COL_VB + ATT_WIDTH
COL_GB = COL_GA + D_MODEL
COL_FA = COL_GB + D_MODEL
PROJ_COLS = COL_FA + LANES


def _const_spec(shape):
    nd = len(shape)
    return pl.BlockSpec(shape, lambda *_: (0,) * nd, pipeline_mode=pl.Buffered(1))


def _params(*sem):
    return pltpu.CompilerParams(dimension_semantics=sem, vmem_limit_bytes=VMEM_LIMIT)


def _rms(x, g):
    return x * lax.rsqrt(jnp.mean(x * x, axis=-1, keepdims=True) + EPS) * g


def _swiglu(xn, win_ref, wout_ref):
    gate = jnp.dot(xn, win_ref[:, :D_FF], preferred_element_type=F32)
    up = jnp.dot(xn, win_ref[:, D_FF:], preferred_element_type=F32)
    act = (gate * jax.nn.sigmoid(gate) * up).astype(BF16)
    return jnp.dot(act, wout_ref[...], preferred_element_type=F32)


W_STEPS = 11
FFN_SLABS = 11


def _slab_spec(shape, axis, steps, **kw):
    block = tuple(d // steps if a == axis else d for a, d in enumerate(shape))
    assert block[axis] * steps == shape[axis] and steps <= W_STEPS
    if axis == 0:
        return pl.BlockSpec(block, lambda i: (jnp.minimum(i, steps - 1), 0), **kw)
    return pl.BlockSpec(block, lambda i: (0, jnp.minimum(i, steps - 1)), **kw)


def _stage(step, src_ref, dst_ref, axis):
    n = src_ref.shape[axis]
    for c in range(dst_ref.shape[axis] // n):
        @pl.when(step == c)
        def _(c=c):
            if axis == 0:
                dst_ref[c * n:(c + 1) * n, :] = src_ref[...].astype(BF16)
            else:
                dst_ref[:, c * n:(c + 1) * n] = src_ref[...].astype(BF16)


ROW_TILE = 512


def _row_tiles(first, count, cols=D_MODEL, col=0, **kw):
    return pl.BlockSpec((ROW_TILE, cols), lambda i: (jnp.clip(i - W_STEPS - first, 0, count - 1), col), **kw)


def _row_halves():
    half = ROW_TILE // 2
    return slice(0, half), slice(half, ROW_TILE)


def _ffn1_kernel(xp_ref, xs_ref, g1_ref, win_ref, wout_ref, gmix_ref, h_ref, n_ref, win_s, wout_s, *,
                 n_prompt):
    step = pl.program_id(0)
    _stage(step, win_ref, win_s, 1)
    _stage(step, wout_ref, wout_s, 0)

    def tile(x_ref):
        for rows in _row_halves():
            x = x_ref[rows, :]
            xn = _rms(x, g1_ref[...]).astype(BF16)
            h = x + 0.5 * _swiglu(xn, win_s, wout_s)
            h_ref[rows, :] = h
            n_ref[rows, :] = _rms(h, gmix_ref[...]).astype(BF16)

    @pl.when((step >= W_STEPS) & (step < W_STEPS + n_prompt))
    def _():
        tile(xp_ref)

    @pl.when(step >= W_STEPS + n_prompt)
    def _():
        tile(xs_ref)


def _ffn1(xp, xs, g1, win, wout, gmix):
    n_prompt, n_sample = xp.shape[0] // ROW_TILE, xs.shape[0] // ROW_TILE
    tiles = n_prompt + n_sample
    m = tiles * ROW_TILE
    return pl.pallas_call(
        functools.partial(_ffn1_kernel, n_prompt=n_prompt),
        grid=(W_STEPS + tiles,),
        in_specs=[_row_tiles(0, n_prompt), _row_tiles(n_prompt, n_sample, pipeline_mode=pl.Buffered(1)),
                  _const_spec(g1.shape), _slab_spec(win.shape, 1, FFN_SLABS),
                  _slab_spec(wout.shape, 0, FFN_SLABS), _const_spec(gmix.shape)],
        out_specs=[_row_tiles(0, tiles), _row_tiles(0, tiles)],
        out_shape=[jax.ShapeDtypeStruct((m, D_MODEL), F32), jax.ShapeDtypeStruct((m, D_MODEL), BF16)],
        scratch_shapes=[pltpu.VMEM(win.shape, BF16), pltpu.VMEM(wout.shape, BF16)],
        compiler_params=_params("arbitrary"),
        name="ffn1",
    )(xp, xs, g1, win, wout, gmix)


PROJ_STEP = 1024


W_IN_SLAB = 1024
W_IN_GATE0 = COL_QB


def _stage_w_in(step, src_ref, dst_ref, n_cols):
    n = src_ref.shape[0]
    g0, g1 = W_IN_GATE0, W_IN_GATE0 + GLA_GATE_RANK
    moves = [(0, g0, 0), (g0, g1, COL_FA), (g1, n_cols, g0)]
    for c in range(pl.cdiv(n_cols, n)):
        @pl.when(step == c)
        def _(c=c):
            lo, hi = c * n, min((c + 1) * n, n_cols)
            for first, end, new in moves:
                a, b = max(lo, first), min(hi, end)
                if a < b:
                    dst_ref[new + a - first:new + b - first, :] = src_ref[a - lo:b - lo, :].astype(BF16)
            if c == 0:
                pad = dst_ref.shape[0] - COL_FA - GLA_GATE_RANK
                dst_ref[COL_FA + GLA_GATE_RANK:, :] = jnp.zeros((pad, dst_ref.shape[1]), BF16)


def _proj_kernel(n_ref, w_ref, o_ref, vt_ref, kp_ref, vp_ref, w_s, *, n_prompt, seq_tiles, n_cols):
    step = pl.program_id(0)
    _stage_w_in(step, w_ref, w_s, n_cols)

    @pl.when(step >= W_STEPS)
    def _():
        tile = step - W_STEPS
        n = n_ref[...]
        kept = {}
        for c0 in range(0, PROJ_COLS, PROJ_STEP):
            c1 = min(c0 + PROJ_STEP, PROJ_COLS)
            acc = lax.dot_general(n, w_s[c0:c1, :], (((1,), (1,)), ((), ())), preferred_element_type=F32)
            o_ref[:, c0:c1] = acc.astype(BF16)
            if c0 in (COL_KB, COL_VB):
                kept[c0] = acc
        v_t = kept[COL_VB].T
        vt_ref[...] = v_t.astype(BF16)

        @pl.when((tile < n_prompt) & (tile % seq_tiles == seq_tiles - 1))
        def _():
            kp_ref[...] = kept[COL_KB].T.reshape(kp_ref.shape)
            vp_ref[...] = v_t.reshape(vp_ref.shape)


def _proj(n, w_t, bp, tp):
    tiles = n.shape[0] // ROW_TILE
    n_prompt, seq_tiles = bp * tp // ROW_TILE, tp // ROW_TILE
    n_cols = w_t.shape[0]
    assert PROJ_STEP == ATT_WIDTH and min(ATT_PAST, tp) == ROW_TILE and pl.cdiv(n_cols, W_IN_SLAB) <= W_STEPS
    last_slab = pl.cdiv(n_cols, W_IN_SLAB) - 1
    kept = pl.BlockSpec((None, None, ATT_HEADS, ATT_DH, ROW_TILE),
                        lambda i: (0, jnp.clip((i - W_STEPS) // seq_tiles, 0, bp - 1), 0, 0, 0))
    return pl.pallas_call(
        functools.partial(_proj_kernel, n_prompt=n_prompt, seq_tiles=seq_tiles, n_cols=n_cols),
        grid=(W_STEPS + tiles,),
        in_specs=[_row_tiles(0, tiles),
                  pl.BlockSpec((W_IN_SLAB, D_MODEL), lambda i: (jnp.minimum(i, last_slab), 0))],
        out_specs=[_row_tiles(0, tiles, cols=PROJ_COLS),
                   pl.BlockSpec((ATT_WIDTH, ROW_TILE), lambda i: (0, jnp.clip(i - W_STEPS, 0, tiles - 1))),
                   kept, kept],
        out_shape=[jax.ShapeDtypeStruct((tiles * ROW_TILE, PROJ_COLS), BF16),
                   jax.ShapeDtypeStruct((ATT_WIDTH, tiles * ROW_TILE), BF16)]
        + [jax.ShapeDtypeStruct((1, bp, ATT_HEADS, ATT_DH, ROW_TILE), F32)] * 2,
        scratch_shapes=[pltpu.VMEM((PROJ_COLS, D_MODEL), BF16)],
        compiler_params=_params("arbitrary"),
        name="proj",
    )(n, w_t)


GLA_ROWS = 2048


GLA_GROUP = 2


def _gla_head(q, k, v, r, f, wg, bg, gn, s0s):
    cl = CHUNK
    nc = q.shape[0] // cl
    per_seq = nc // len(s0s)
    bmm = functools.partial(jnp.einsum, preferred_element_type=F32)

    z = jnp.dot(f, wg, preferred_element_type=F32) + bg
    log_a = (jnp.minimum(z, 0.0) - jnp.log(1.0 + jnp.exp(-jnp.abs(z)))) * (LOG2E / GLA_TAU)
    log_a = log_a.reshape(nc, cl, GLA_DK)

    ri = lax.broadcasted_iota(jnp.int32, (cl, cl), 0)
    ci = lax.broadcasted_iota(jnp.int32, (cl, cl), 1)
    causal = ri >= ci
    tri = jnp.broadcast_to(causal.astype(BF16), (nc, cl, cl))
    hi = log_a.astype(BF16)
    lo = (log_a - hi.astype(F32)).astype(BF16)
    b = bmm('nij,njd->nid', tri, hi) + bmm('nij,njd->nid', tri, lo)
    b_last = b[:, cl - 1:cl, :]

    q = q.astype(F32).reshape(nc, cl, GLA_DK) * (GLA_DK ** -0.5)
    k = k.astype(F32).reshape(nc, cl, GLA_DK)
    v = v.reshape(nc, cl, GLA_DV)
    q_dec = (q * jnp.exp2(b)).astype(BF16)
    k_inv = (k * jnp.exp2(-b)).astype(BF16)
    k_end = (k * jnp.exp2(b_last - b)).astype(BF16)

    scores = jnp.where(causal, bmm('nid,njd->nij', q_dec, k_inv), 0.0)
    o_intra = bmm('nij,njv->niv', scores.astype(BF16), v)
    incr_t = bmm('njv,njd->nvd', v, k_end)
    decay = jnp.exp2(b_last)

    before, states = [], []
    for i, s0 in enumerate(s0s):
        s = s0.T
        for c in range(i * per_seq, (i + 1) * per_seq):
            before.append(s.astype(BF16))
            s = s * decay[c] + incr_t[c]
        states.append(s.T)
    s_before = jnp.stack(before)
    o = o_intra + bmm('nid,nvd->niv', q_dec, s_before)

    o = o * lax.rsqrt(jnp.mean(o * o, axis=-1, keepdims=True) + EPS) * gn
    r = r.astype(F32)
    return o.reshape(nc * cl, GLA_DV) * (r * jax.nn.sigmoid(r)), states


def _gla_kernel(q_ref, k_ref, v_ref, r_ref, f_ref, wg_ref, bg_ref, gn_ref, s0_ref, o_ref, s_ref):
    n_seq = s0_ref.shape[0]
    for j in range(GLA_GROUP):
        dk = slice(j * GLA_DK, (j + 1) * GLA_DK)
        dv = slice(j * GLA_DV, (j + 1) * GLA_DV)
        o, states = _gla_head(q_ref[:, dk], k_ref[:, dk], v_ref[:, dv], r_ref[:, dv], f_ref[...],
                              wg_ref[:, dk], bg_ref[:, dk], gn_ref[:, dv],
                              [s0_ref[i, j] for i in range(n_seq)])
        o_ref[:, dv] = o.astype(BF16)
        for i, s in enumerate(states):
            s_ref[i, j] = s


def _gla(p, wg, bg, gn, s0, batch, t, row0):
    wk, wv = GLA_GROUP * GLA_DK, GLA_GROUP * GLA_DV
    cq, ck, cv, cr = COL_QA // wk, COL_KA // wk, COL_VA // wv, COL_RA // wv
    n_seq = min(batch, max(1, GLA_ROWS // t))
    rows = n_seq * t
    assert row0 % rows == 0 and batch % n_seq == 0
    b0 = row0 // rows
    state = pl.BlockSpec((n_seq, GLA_GROUP, GLA_DK, GLA_DV), lambda b, g: (b, g, 0, 0))
    return pl.pallas_call(
        _gla_kernel,
        grid=(batch // n_seq, GLA_HEADS // GLA_GROUP),
        in_specs=[
            pl.BlockSpec((rows, wk), lambda b, g: (b0 + b, cq + g)),
            pl.BlockSpec((rows, wk), lambda b, g: (b0 + b, ck + g)),
            pl.BlockSpec((rows, wv), lambda b, g: (b0 + b, cv + g)),
            pl.BlockSpec((rows, wv), lambda b, g: (b0 + b, cr + g)),
            pl.BlockSpec((rows, LANES), lambda b, g: (b0 + b, COL_FA // LANES)),
            pl.BlockSpec((LANES, wk), lambda b, g: (0, g)),
            pl.BlockSpec((1, wk), lambda b, g: (0, g)),
            pl.BlockSpec((1, wv), lambda b, g: (0, g)),
            state,
        ],
        out_specs=[pl.BlockSpec((rows, wv), lambda b, g: (b, g)), state],
        out_shape=[jax.ShapeDtypeStruct((batch * t, GLA_V_WIDTH), BF16),
                   jax.ShapeDtypeStruct((batch, GLA_HEADS, GLA_DK, GLA_DV), F32)],
        compiler_params=_params("parallel", "parallel"),
        name="gla",
    )(p, p, p, p, p, wg, bg, gn, s0)


ATT_GROUP = 4
ATT_LANES = ATT_GROUP * ATT_DH
ATT_TQ = 256
REL_RING = 1024


def _rel_row(table):
    u = np.arange(REL_RING)
    u = np.where(u < ATT_PAST + ATT_TQ, u, u - REL_RING)
    idx = np.clip(ATT_PAST - u, -REL_CLIP, REL_CLIP) + REL_CLIP
    return table[:, idx].astype(F32)[:, None, :]


def _build_bias(rel_ref, bias_ref):
    _, tq, nk = bias_ref.shape
    qc = lax.broadcasted_iota(jnp.int32, (tq, nk), 0) // CHUNK
    kc = lax.broadcasted_iota(jnp.int32, (tq, nk), 1) // CHUNK
    visible = (kc >= qc) & (kc <= qc + ATT_PAST // CHUNK)
    for hh in range(bias_ref.shape[0]):
        row = jnp.broadcast_to(rel_ref[hh], (tq, REL_RING))
        toeplitz = pltpu.roll(row, 0, 1, stride=1, stride_axis=0)
        bias_ref[hh] = jnp.where(visible, toeplitz[:, :nk] * LOG2E, NEG_INF)


def _head_lanes(hh):
    lane = lax.broadcasted_iota(jnp.int32, (1, ATT_LANES), 1)
    return (lane >= hh * ATT_DH) & (lane < (hh + 1) * ATT_DH)


SUM_ROWS = 16


def _head_values_t(v_t):
    ones = jnp.ones((SUM_ROWS, v_t.shape[1]), v_t.dtype)
    return [jnp.concatenate([v_t[hh * ATT_DH:(hh + 1) * ATT_DH, :], ones], axis=0) for hh in range(ATT_GROUP)]


def _band_exp(s, bias_ref, hh, boff):
    tq, nk = s.shape
    half = tq // 2

    def part(rows, c0, c1):
        x = s[rows, c0:c1] + bias_ref[hh, rows, boff + c0:boff + c1]
        ex = jnp.exp2(x - jnp.max(x, axis=-1, keepdims=True)).astype(BF16)
        left = [jnp.zeros((half, c0), BF16)] if c0 else []
        right = [jnp.zeros((half, nk - c1), BF16)] if c1 < nk else []
        return jnp.concatenate(left + [ex] + right, axis=1) if left or right else ex

    top = part(slice(0, half), 0, min(nk, ATT_PAST + half - boff))
    bottom = part(slice(half, tq), max(half - boff, 0), nk)
    return jnp.concatenate([top, bottom], axis=0)


def _attend(q, k, vhs_t, bias_ref, boff):
    nk = k.shape[0]
    nt = (((1,), (1,)), ((), ()))
    q = q.astype(F32) * (ATT_DH ** -0.5 * LOG2E)
    outs_t = []
    for hh in range(ATT_GROUP):
        qh = jnp.where(_head_lanes(hh), q, 0.0).astype(BF16)
        s = lax.dot_general(qh, k, nt, preferred_element_type=F32)
        e = _band_exp(s, bias_ref, hh, boff)
        pv_t = lax.dot_general(vhs_t[hh], e, nt, preferred_element_type=F32)
        outs_t.append(pv_t[:ATT_DH] / pv_t[ATT_DH:ATT_DH + 1])
    return jnp.concatenate(outs_t, axis=0).T.astype(BF16)


def _band_prompt_kernel(rel_ref, q_ref, k_ref, vt_ref, o_ref, bias_ref):
    _, tq, nk = bias_ref.shape
    n_blocks = q_ref.shape[0] // tq
    n_past = ATT_PAST // tq

    @pl.when(pl.program_id(1) == 0)
    def _():
        _build_bias(rel_ref, bias_ref)

    vhs_t = _head_values_t(vt_ref[...])
    for tt in range(n_blocks):
        rows = slice(tt * tq, (tt + 1) * tq)
        keys = slice(max(tt - n_past, 0) * tq, (tt + 1) * tq)
        o_ref[rows, :] = _attend(q_ref[rows, :], k_ref[keys, :], [vh[:, keys] for vh in vhs_t], bias_ref,
                                 max(n_past - tt, 0) * tq)


def _band_prompt(p, v_t, rel, batch, t):
    w = ATT_LANES
    cq, ck = COL_QB // w, COL_KB // w
    return pl.pallas_call(
        _band_prompt_kernel,
        grid=(ATT_HEADS // ATT_GROUP, batch),
        in_specs=[
            pl.BlockSpec((ATT_GROUP, 1, REL_RING), lambda g, b: (g, 0, 0)),
            pl.BlockSpec((t, w), lambda g, b: (b, cq + g)),
            pl.BlockSpec((t, w), lambda g, b: (b, ck + g)),
            pl.BlockSpec((w, t), lambda g, b: (g, b)),
        ],
        out_specs=pl.BlockSpec((t, w), lambda g, b: (b, g)),
        out_shape=jax.ShapeDtypeStruct((batch * t, ATT_WIDTH), BF16),
        scratch_shapes=[pltpu.VMEM((ATT_GROUP, ATT_TQ, ATT_PAST + ATT_TQ), F32)],
        compiler_params=_params("arbitrary", "arbitrary"),
        name="band_prompt",
    )(rel, p, p, v_t)


def _attend_cached(q, kc_t, vc_t, k_new, v_new, bias):
    tq, c = q.shape[0], kc_t.shape[1]
    nt = (((1,), (1,)), ((), ()))
    q = q.astype(F32) * (ATT_DH ** -0.5 * LOG2E)
    qs = jnp.concatenate([jnp.where(_head_lanes(hh), q, 0.0) for hh in range(ATT_GROUP)], axis=0).astype(BF16)
    s = jnp.concatenate([jnp.dot(qs, kc_t, preferred_element_type=F32),
                         lax.dot_general(qs, k_new, nt, preferred_element_type=F32)], axis=1) + bias
    e = jnp.exp2(s - jnp.max(s, axis=-1, keepdims=True))
    eb = e.astype(BF16)
    pv = (lax.dot_general(eb[:, :c], vc_t, nt, preferred_element_type=F32)
          + jnp.dot(eb[:, c:], v_new, preferred_element_type=F32)) / jnp.sum(e, axis=-1, keepdims=True)
    out = pv[:tq]
    for hh in range(1, ATT_GROUP):
        out = jnp.where(_head_lanes(hh), pv[hh * tq:(hh + 1) * tq], out)
    return out.astype(BF16)


def _band_step_kernel(rel_ref, q_ref, k_ref, v_ref, ck_ref, cv_ref, o_ref, bias_ref):
    @pl.when(pl.program_id(0) == 0)
    def _():
        _build_bias(rel_ref, bias_ref)

    t, nk = bias_ref.shape[1:]
    c = ck_ref.shape[2]
    kc_t = ck_ref[...].reshape(ATT_WIDTH, c).astype(BF16)
    vc_t = cv_ref[...].reshape(ATT_WIDTH, c).astype(BF16)
    for g in range(ATT_HEADS // ATT_GROUP):
        lanes = slice(g * ATT_LANES, (g + 1) * ATT_LANES)
        bias = bias_ref[g * ATT_GROUP:(g + 1) * ATT_GROUP].reshape(ATT_GROUP * t, nk)
        o_ref[:, lanes] = _attend_cached(q_ref[:, lanes], kc_t[lanes, :], vc_t[lanes, :], k_ref[:, lanes],
                                         v_ref[:, lanes], bias)


def _band_step(p, cache_k_t, cache_v_t, rel, batch, t, row0):
    c = cache_k_t.shape[4]
    assert row0 % t == 0
    r0 = row0 // t
    new = lambda col: pl.BlockSpec((t, ATT_WIDTH), lambda b: (r0 + b, col // ATT_WIDTH))
    cache = pl.BlockSpec((None, None, ATT_HEADS, ATT_DH, c), lambda b: (0, b, 0, 0, 0))
    return pl.pallas_call(
        _band_step_kernel,
        grid=(batch,),
        in_specs=[_const_spec(rel.shape), new(COL_QB), new(COL_KB), new(COL_VB), cache, cache],
        out_specs=pl.BlockSpec((t, ATT_WIDTH), lambda b: (b, 0)),
        out_shape=jax.ShapeDtypeStruct((batch * t, ATT_WIDTH), BF16),
        scratch_shapes=[pltpu.VMEM((ATT_HEADS, t, c + t), F32)],
        compiler_params=_params("arbitrary"),
        name="band_step",
    )(rel, p, p, p, cache_k_t, cache_v_t)


SQUARE_STEPS = 8


def _mix_ffn2_kernel(h_ref, oap_ref, obp_ref, oas_ref, obs_ref, ga_ref, gb_ref, wbg_ref, wba_ref, wo_ref,
                     g2_ref, win_ref, wout_ref, gf_ref, yp_ref, ys_ref, wbg_s, wba_s, wo_s, win_s, wout_s,
                     *, n_prompt):
    step = pl.program_id(0)
    _stage(step, wbg_ref, wbg_s, 0)
    _stage(step, wba_ref, wba_s, 0)
    _stage(step, wo_ref, wo_s, 0)
    _stage(step, win_ref, win_s, 1)
    _stage(step, wout_ref, wout_s, 0)

    def tile(oa_ref, ob_ref, y_ref):
        a = jnp.dot(oa_ref[...], wbg_s[...], preferred_element_type=F32)
        b = jnp.dot(ob_ref[...], wba_s[...], preferred_element_type=F32)
        mixed = (jax.nn.sigmoid(ga_ref[...].astype(F32)) * a
                 + jax.nn.sigmoid(gb_ref[...].astype(F32)) * b)
        h = h_ref[...] + jnp.dot(mixed.astype(BF16), wo_s[...], preferred_element_type=F32)
        xn = _rms(h, g2_ref[...]).astype(BF16)
        h = h + 0.5 * _swiglu(xn, win_s, wout_s)
        y_ref[...] = _rms(h, gf_ref[...])

    @pl.when((step >= W_STEPS) & (step < W_STEPS + n_prompt))
    def _():
        tile(oap_ref, obp_ref, yp_ref)

    @pl.when(step >= W_STEPS + n_prompt)
    def _():
        tile(oas_ref, obs_ref, ys_ref)


def _mix_ffn2(h, oa_p, ob_p, oa_s, ob_s, p, wbg, wba, wo, g2, win, wout, gf):
    tiles = h.shape[0] // ROW_TILE
    n_prompt, n_sample = oa_p.shape[0] // ROW_TILE, oa_s.shape[0] // ROW_TILE
    assert tiles == n_prompt + n_sample
    once = dict(pipeline_mode=pl.Buffered(1))
    prompt, sample = _row_tiles(0, n_prompt), _row_tiles(n_prompt, n_sample, **once)
    square = _slab_spec(wbg.shape, 0, SQUARE_STEPS, **once)
    return pl.pallas_call(
        functools.partial(_mix_ffn2_kernel, n_prompt=n_prompt),
        grid=(W_STEPS + tiles,),
        in_specs=[_row_tiles(0, tiles), prompt, prompt, sample, sample,
                  _row_tiles(0, tiles, col=COL_GA // D_MODEL), _row_tiles(0, tiles, col=COL_GB // D_MODEL),
                  square, square, square, _const_spec(g2.shape), _slab_spec(win.shape, 1, FFN_SLABS),
                  _slab_spec(wout.shape, 0, FFN_SLABS), _const_spec(gf.shape)],
        out_specs=[prompt, sample],
        out_shape=[jax.ShapeDtypeStruct((n_prompt * ROW_TILE, D_MODEL), F32),
                   jax.ShapeDtypeStruct((n_sample * ROW_TILE, D_MODEL), F32)],
        scratch_shapes=[pltpu.VMEM(wbg.shape, BF16), pltpu.VMEM(wba.shape, BF16), pltpu.VMEM(wo.shape, BF16),
                        pltpu.VMEM(win.shape, BF16), pltpu.VMEM(wout.shape, BF16)],
        compiler_params=_params("arbitrary"),
        name="mix_ffn2",
    )(h, oa_p, ob_p, oa_s, ob_s, p, p, wbg, wba, wo, g2, win, wout, gf)


def _kept_rows(p, row0, batch, t, col):
    keep = min(ATT_PAST, t)
    rows = p[row0:row0 + batch * t, col:col + ATT_WIDTH].reshape(batch, t, ATT_WIDTH)[:, t - keep:]
    return rows.astype(F32).reshape(1, batch, keep, ATT_HEADS, ATT_DH)


def kernel(x_prompt, x_sample, cache_att_k, cache_att_v, state_gla, norm_ffn1, w_ffn1_in, w_ffn1_out,
           norm_mix, w_in, w_gla_gate, b_gla_gate, gla_norm, attn_rel_bias, w_branch_gla, w_branch_att,
           w_out, norm_ffn2, w_ffn2_in, w_ffn2_out, norm_final):
    assert norm_ffn1.shape[0] == 1, "single layer"
    bp, tp, _ = x_prompt.shape
    bs, ts, _ = x_sample.shape
    mp, ms = bp * tp, bs * ts
    assert mp % ROW_TILE == 0 and ms % ROW_TILE == 0
    assert cache_att_k.shape[2] == ATT_PAST and ts <= ATT_TQ

    wg = jnp.concatenate([w_gla_gate[0], jnp.zeros((LANES - GLA_GATE_RANK, GLA_K_WIDTH), F32)],
                         axis=0).astype(BF16)
    rel = _rel_row(attn_rel_bias[0])

    h, n = _ffn1(x_prompt.reshape(mp, D_MODEL), x_sample.reshape(ms, D_MODEL), norm_ffn1, w_ffn1_in[0],
                 w_ffn1_out[0], norm_mix)
    p, v_t, k_prompt_t, v_prompt_t = _proj(n, w_in[0].T, bp, tp)
    to_rows_last = lambda a: jnp.transpose(a, (0, 1, 3, 4, 2))
    to_rows_first = lambda a: jnp.transpose(a, (0, 1, 4, 2, 3))
    gla_w = (wg, b_gla_gate, gla_norm)
    oa_p, s_prompt = _gla(p, *gla_w, jnp.zeros((bp, GLA_HEADS, GLA_DK, GLA_DV), F32), bp, tp, 0)
    oa_s, s_sample = _gla(p, *gla_w, state_gla[0], bs, ts, mp)
    ob_p = _band_prompt(p, v_t, rel, bp, tp)
    ob_s = _band_step(p, to_rows_last(cache_att_k), to_rows_last(cache_att_v), rel, bs, ts, mp)
    yp, ys = _mix_ffn2(h, oa_p, ob_p, oa_s, ob_s, p, w_branch_gla[0], w_branch_att[0], w_out[0], norm_ffn2,
                       w_ffn2_in[0], w_ffn2_out[0], norm_final.reshape(1, D_MODEL))
    return (yp.reshape(bp, tp, D_MODEL), ys.reshape(bs, ts, D_MODEL), to_rows_first(k_prompt_t),
            to_rows_first(v_prompt_t), s_prompt[None],
            _kept_rows(p, mp, bs, ts, COL_KB), _kept_rows(p, mp, bs, ts, COL_VB), s_sample[None])
```

```python
import functools

import jax
import jax.numpy as jnp
import numpy as np
from jax import lax
from jax.experimental import pallas as pl
from jax.experimental.pallas import tpu as pltpu

F32 = jnp.float32
BF16 = jnp.bfloat16

D_MODEL = 1024
CHUNK = 64
GLA_HEADS = 4
GLA_DK = 128
GLA_DV = 256
GLA_K_WIDTH = GLA_HEADS * GLA_DK
GLA_V_WIDTH = GLA_HEADS * GLA_DV
GLA_GATE_RANK = 16
GLA_TAU = 16.0
ATT_HEADS = 16
ATT_DH = 64
ATT_WIDTH = ATT_HEADS * ATT_DH
ATT_PAST = 8 * CHUNK
REL_CLIP = 128
D_FF = 2816
EPS = 1e-6
NEG_INF = -1e30
LOG2E = 1.4426950408889634

LANES = 128
VMEM_LIMIT = 60 * 1024 * 1024

COL_QA = 0
COL_KA = COL_QA + GLA_K_WIDTH
COL_VA = COL_KA + GLA_K_WIDTH
COL_RA = COL_VA + GLA_V_WIDTH
COL_QB = COL_RA + GLA_V_WIDTH
COL_KB = COL_QB + ATT_WIDTH
COL_VB = COL_KB + ATT_WIDTH
COL_GA = COL_VB + ATT_WIDTH
COL_GB = COL_GA + D_MODEL
COL_FA = COL_GB + D_MODEL
PROJ_COLS = COL_FA + LANES


def _const_spec(shape):
    nd = len(shape)
    return pl.BlockSpec(shape, lambda *_: (0,) * nd, pipeline_mode=pl.Buffered(1))


def _params(*sem):
    return pltpu.CompilerParams(dimension_semantics=sem, vmem_limit_bytes=VMEM_LIMIT)


def _rms(x, g):
    return x * lax.rsqrt(jnp.mean(x * x, axis=-1, keepdims=True) + EPS) * g


def _swiglu(xn, win_ref, wout_ref):
    gate = jnp.dot(xn, win_ref[:, :D_FF], preferred_element_type=F32)
    up = jnp.dot(xn, win_ref[:, D_FF:], preferred_element_type=F32)
    act = (gate * jax.nn.sigmoid(gate) * up).astype(BF16)
    return jnp.dot(act, wout_ref[...], preferred_element_type=F32)


W_STEPS = 11
FFN_SLABS = 11


def _slab_spec(shape, axis, steps, **kw):
    block = tuple(d // steps if a == axis else d for a, d in enumerate(shape))
    assert block[axis] * steps == shape[axis] and steps <= W_STEPS
    if axis == 0:
        return pl.BlockSpec(block, lambda i: (jnp.minimum(i, steps - 1), 0), **kw)
    return pl.BlockSpec(block, lambda i: (0, jnp.minimum(i, steps - 1)), **kw)


def _stage(step, src_ref, dst_ref, axis):
    n = src_ref.shape[axis]
    for c in range(dst_ref.shape[axis] // n):
        @pl.when(step == c)
        def _(c=c):
            if axis == 0:
                dst_ref[c * n:(c + 1) * n, :] = src_ref[...].astype(BF16)
            else:
                dst_ref[:, c * n:(c + 1) * n] = src_ref[...].astype(BF16)


ROW_TILE = 512


def _row_tiles(first, count, cols=D_MODEL, col=0, **kw):
    return pl.BlockSpec((ROW_TILE, cols), lambda i: (jnp.clip(i - W_STEPS - first, 0, count - 1), col), **kw)


def _row_halves():
    half = ROW_TILE // 2
    return slice(0, half), slice(half, ROW_TILE)


def _ffn1_kernel(xp_ref, xs_ref, g1_ref, win_ref, wout_ref, gmix_ref, h_ref, n_ref, win_s, wout_s, *,
                 n_prompt):
    step = pl.program_id(0)
    _stage(step, win_ref, win_s, 1)
    _stage(step, wout_ref, wout_s, 0)

    def tile(x_ref):
        for rows in _row_halves():
            x = x_ref[rows, :]
            xn = _rms(x, g1_ref[...]).astype(BF16)
            h = x + 0.5 * _swiglu(xn, win_s, wout_s)
            h_ref[rows, :] = h
            n_ref[rows, :] = _rms(h, gmix_ref[...]).astype(BF16)

    @pl.when((step >= W_STEPS) & (step < W_STEPS + n_prompt))
    def _():
        tile(xp_ref)

    @pl.when(step >= W_STEPS + n_prompt)
    def _():
        tile(xs_ref)


def _ffn1(xp, xs, g1, win, wout, gmix):
    n_prompt, n_sample = xp.shape[0] // ROW_TILE, xs.shape[0] // ROW_TILE
    tiles = n_prompt + n_sample
    m = tiles * ROW_TILE
    return pl.pallas_call(
        functools.partial(_ffn1_kernel, n_prompt=n_prompt),
        grid=(W_STEPS + tiles,),
        in_specs=[_row_tiles(0, n_prompt), _row_tiles(n_prompt, n_sample, pipeline_mode=pl.Buffered(1)),
                  _const_spec(g1.shape), _slab_spec(win.shape, 1, FFN_SLABS),
                  _slab_spec(wout.shape, 0, FFN_SLABS), _const_spec(gmix.shape)],
        out_specs=[_row_tiles(0, tiles), _row_tiles(0, tiles)],
        out_shape=[jax.ShapeDtypeStruct((m, D_MODEL), F32), jax.ShapeDtypeStruct((m, D_MODEL), BF16)],
        scratch_shapes=[pltpu.VMEM(win.shape, BF16), pltpu.VMEM(wout.shape, BF16)],
        compiler_params=_params("arbitrary"),
        name="ffn1",
    )(xp, xs, g1, win, wout, gmix)


PROJ_STEP = 1024


W_IN_SLAB = 1024
W_IN_GATE0 = COL_QB


def _stage_w_in(step, src_ref, dst_ref, n_cols):
    n = src_ref.shape[0]
    g0, g1 = W_IN_GATE0, W_IN_GATE0 + GLA_GATE_RANK
    moves = [(0, g0, 0), (g0, g1, COL_FA), (g1, n_cols, g0)]
    for c in range(pl.cdiv(n_cols, n)):
        @pl.when(step == c)
        def _(c=c):
            lo, hi = c * n, min((c + 1) * n, n_cols)
            for first, end, new in moves:
                a, b = max(lo, first), min(hi, end)
                if a < b:
                    dst_ref[new + a - first:new + b - first, :] = src_ref[a - lo:b - lo, :].astype(BF16)
            if c == 0:
                pad = dst_ref.shape[0] - COL_FA - GLA_GATE_RANK
                dst_ref[COL_FA + GLA_GATE_RANK:, :] = jnp.zeros((pad, dst_ref.shape[1]), BF16)


def _proj_kernel(n_ref, w_ref, o_ref, vt_ref, kp_ref, vp_ref, w_s, *, n_prompt, seq_tiles, n_cols):
    step = pl.program_id(0)
    _stage_w_in(step, w_ref, w_s, n_cols)

    @pl.when(step >= W_STEPS)
    def _():
        tile = step - W_STEPS
        n = n_ref[...]
        kept = {}
        for c0 in range(0, PROJ_COLS, PROJ_STEP):
            c1 = min(c0 + PROJ_STEP, PROJ_COLS)
            acc = lax.dot_general(n, w_s[c0:c1, :], (((1,), (1,)), ((), ())), preferred_element_type=F32)
            o_ref[:, c0:c1] = acc.astype(BF16)
            if c0 in (COL_KB, COL_VB):
                kept[c0] = acc
        v_t = kept[COL_VB].T
        vt_ref[...] = v_t.astype(BF16)

        @pl.when((tile < n_prompt) & (tile % seq_tiles == seq_tiles - 1))
        def _():
            kp_ref[...] = kept[COL_KB].T.reshape(kp_ref.shape)
            vp_ref[...] = v_t.reshape(vp_ref.shape)


def _proj(n, w_t, bp, tp):
    tiles = n.shape[0] // ROW_TILE
    n_prompt, seq_tiles = bp * tp // ROW_TILE, tp // ROW_TILE
    n_cols = w_t.shape[0]
    assert PROJ_STEP == ATT_WIDTH and min(ATT_PAST, tp) == ROW_TILE and pl.cdiv(n_cols, W_IN_SLAB) <= W_STEPS
    last_slab = pl.cdiv(n_cols, W_IN_SLAB) - 1
    kept = pl.BlockSpec((None, None, ATT_HEADS, ATT_DH, ROW_TILE),
                        lambda i: (0, jnp.clip((i - W_STEPS) // seq_tiles, 0, bp - 1), 0, 0, 0))
    return pl.pallas_call(
        functools.partial(_proj_kernel, n_prompt=n_prompt, seq_tiles=seq_tiles, n_cols=n_cols),
        grid=(W_STEPS + tiles,),
        in_specs=[_row_tiles(0, tiles),
                  pl.BlockSpec((W_IN_SLAB, D_MODEL), lambda i: (jnp.minimum(i, last_slab), 0))],
        out_specs=[_row_tiles(0, tiles, cols=PROJ_COLS),
                   pl.BlockSpec((ATT_WIDTH, ROW_TILE), lambda i: (0, jnp.clip(i - W_STEPS, 0, tiles - 1))),
                   kept, kept],
        out_shape=[jax.ShapeDtypeStruct((tiles * ROW_TILE, PROJ_COLS), BF16),
                   jax.ShapeDtypeStruct((ATT_WIDTH, tiles * ROW_TILE), BF16)]
        + [jax.ShapeDtypeStruct((1, bp, ATT_HEADS, ATT_DH, ROW_TILE), F32)] * 2,
        scratch_shapes=[pltpu.VMEM((PROJ_COLS, D_MODEL), BF16)],
        compiler_params=_params("arbitrary"),
        name="proj",
    )(n, w_t)


GLA_ROWS = 2048


GLA_GROUP = 2


def _gla_head(q, k, v, r, f, wg, bg, gn, s0s):
    cl = CHUNK
    nc = q.shape[0] // cl
    per_seq = nc // len(s0s)
    bmm = functools.partial(jnp.einsum, preferred_element_type=F32)

    z = jnp.dot(f, wg, preferred_element_type=F32) + bg
    log_a = (jnp.minimum(z, 0.0) - jnp.log(1.0 + jnp.exp(-jnp.abs(z)))) * (LOG2E / GLA_TAU)
    log_a = log_a.reshape(nc, cl, GLA_DK)

    ri = lax.broadcasted_iota(jnp.int32, (cl, cl), 0)
    ci = lax.broadcasted_iota(jnp.int32, (cl, cl), 1)
    causal = ri >= ci
    tri = jnp.broadcast_to(causal.astype(BF16), (nc, cl, cl))
    hi = log_a.astype(BF16)
    lo = (log_a - hi.astype(F32)).astype(BF16)
    b2 = bmm('nij,njd->nid', tri, jnp.concatenate([hi, lo], axis=-1))
    b = b2[..., :GLA_DK] + b2[..., GLA_DK:]
    b_last = b[:, cl - 1:cl, :]

    q = q.astype(F32).reshape(nc, cl, GLA_DK) * (GLA_DK ** -0.5)
    k = k.astype(F32).reshape(nc, cl, GLA_DK)
    v = v.reshape(nc, cl, GLA_DV)
    q_dec = (q * jnp.exp2(b)).astype(BF16)
    k_inv = (k * jnp.exp2(-b)).astype(BF16)
    k_end = (k * jnp.exp2(b_last - b)).astype(BF16)

    scores = jnp.where(causal, bmm('nid,njd->nij', q_dec, k_inv), 0.0)
    o_intra = bmm('nij,njv->niv', scores.astype(BF16), v)
    incr_t = bmm('njv,njd->nvd', v, k_end)
    decay = jnp.exp2(b_last)

    before, states = [], []
    for i, s0 in enumerate(s0s):
        s = s0.T
        for c in range(i * per_seq, (i + 1) * per_seq):
            before.append(s.astype(BF16))
            s = s * decay[c] + incr_t[c]
        states.append(s.T)
    s_before = jnp.stack(before)
    o = o_intra + bmm('nid,nvd->niv', q_dec, s_before)

    o = o * lax.rsqrt(jnp.mean(o * o, axis=-1, keepdims=True) + EPS) * gn
    r = r.astype(F32)
    return o.reshape(nc * cl, GLA_DV) * (r * jax.nn.sigmoid(r)), states


def _gla_kernel(q_ref, k_ref, v_ref, r_ref, f_ref, wg_ref, bg_ref, gn_ref, s0_ref, o_ref, s_ref):
    n_seq = s0_ref.shape[0]
    for j in range(GLA_GROUP):
        dk = slice(j * GLA_DK, (j + 1) * GLA_DK)
        dv = slice(j * GLA_DV, (j + 1) * GLA_DV)
        o, states = _gla_head(q_ref[:, dk], k_ref[:, dk], v_ref[:, dv], r_ref[:, dv], f_ref[...],
                              wg_ref[:, dk], bg_ref[:, dk], gn_ref[:, dv],
                              [s0_ref[i, j] for i in range(n_seq)])
        o_ref[:, dv] = o.astype(BF16)
        for i, s in enumerate(states):
            s_ref[i, j] = s


def _gla(p, wg, bg, gn, s0, batch, t, row0):
    wk, wv = GLA_GROUP * GLA_DK, GLA_GROUP * GLA_DV
    cq, ck, cv, cr = COL_QA // wk, COL_KA // wk, COL_VA // wv, COL_RA // wv
    n_seq = min(batch, max(1, GLA_ROWS // t))
    rows = n_seq * t
    assert row0 % rows == 0 and batch % n_seq == 0
    b0 = row0 // rows
    state = pl.BlockSpec((n_seq, GLA_GROUP, GLA_DK, GLA_DV), lambda b, g: (b, g, 0, 0))
    return pl.pallas_call(
        _gla_kernel,
        grid=(batch // n_seq, GLA_HEADS // GLA_GROUP),
        in_specs=[
            pl.BlockSpec((rows, wk), lambda b, g: (b0 + b, cq + g)),
            pl.BlockSpec((rows, wk), lambda b, g: (b0 + b, ck + g)),
            pl.BlockSpec((rows, wv), lambda b, g: (b0 + b, cv + g)),
            pl.BlockSpec((rows, wv), lambda b, g: (b0 + b, cr + g)),
            pl.BlockSpec((rows, LANES), lambda b, g: (b0 + b, COL_FA // LANES)),
            pl.BlockSpec((LANES, wk), lambda b, g: (0, g)),
            pl.BlockSpec((1, wk), lambda b, g: (0, g)),
            pl.BlockSpec((1, wv), lambda b, g: (0, g)),
            state,
        ],
        out_specs=[pl.BlockSpec((rows, wv), lambda b, g: (b, g)), state],
        out_shape=[jax.ShapeDtypeStruct((batch * t, GLA_V_WIDTH), BF16),
                   jax.ShapeDtypeStruct((batch, GLA_HEADS, GLA_DK, GLA_DV), F32)],
        compiler_params=_params("parallel", "parallel"),
        name="gla",
    )(p, p, p, p, p, wg, bg, gn, s0)


ATT_GROUP = 4
ATT_LANES = ATT_GROUP * ATT_DH
ATT_TQ = 256
REL_RING = 1024


def _rel_row(table):
    u = np.arange(REL_RING)
    u = np.where(u < ATT_PAST + ATT_TQ, u, u - REL_RING)
    idx = np.clip(ATT_PAST - u, -REL_CLIP, REL_CLIP) + REL_CLIP
    return table[:, idx].astype(F32)[:, None, :]


def _build_bias(rel_ref, bias_ref):
    _, tq, nk = bias_ref.shape
    qc = lax.broadcasted_iota(jnp.int32, (tq, nk), 0) // CHUNK
    kc = lax.broadcasted_iota(jnp.int32, (tq, nk), 1) // CHUNK
    visible = (kc >= qc) & (kc <= qc + ATT_PAST // CHUNK)
    for hh in range(bias_ref.shape[0]):
        row = jnp.broadcast_to(rel_ref[hh], (tq, REL_RING))
        toeplitz = pltpu.roll(row, 0, 1, stride=1, stride_axis=0)
        bias_ref[hh] = jnp.where(visible, toeplitz[:, :nk] * LOG2E, NEG_INF)


def _head_lanes(hh):
    lane = lax.broadcasted_iota(jnp.int32, (1, ATT_LANES), 1)
    return (lane >= hh * ATT_DH) & (lane < (hh + 1) * ATT_DH)


SUM_ROWS = 16


def _head_values_t(v_t):
    ones = jnp.ones((SUM_ROWS, v_t.shape[1]), v_t.dtype)
    return [jnp.concatenate([v_t[hh * ATT_DH:(hh + 1) * ATT_DH, :], ones], axis=0) for hh in range(ATT_GROUP)]


def _band_exp(s, bias_ref, hh, boff):
    tq, nk = s.shape
    half = tq // 2

    def part(rows, c0, c1):
        x = s[rows, c0:c1] + bias_ref[hh, rows, boff + c0:boff + c1]
        ex = jnp.exp2(x - jnp.max(x, axis=-1, keepdims=True)).astype(BF16)
        left = [jnp.zeros((half, c0), BF16)] if c0 else []
        right = [jnp.zeros((half, nk - c1), BF16)] if c1 < nk else []
        return jnp.concatenate(left + [ex] + right, axis=1) if left or right else ex

    top = part(slice(0, half), 0, min(nk, ATT_PAST + half - boff))
    bottom = part(slice(half, tq), max(half - boff, 0), nk)
    return jnp.concatenate([top, bottom], axis=0)


def _attend(q, k, vhs_t, bias_ref, boff):
    nk = k.shape[0]
    nt = (((1,), (1,)), ((), ()))
    q = q.astype(F32) * (ATT_DH ** -0.5 * LOG2E)
    outs_t = []
    for hh in range(ATT_GROUP):
        qh = jnp.where(_head_lanes(hh), q, 0.0).astype(BF16)
        s = lax.dot_general(qh, k, nt, preferred_element_type=F32)
        e = _band_exp(s, bias_ref, hh, boff)
        pv_t = lax.dot_general(vhs_t[hh], e, nt, preferred_element_type=F32)
        outs_t.append(pv_t[:ATT_DH] / pv_t[ATT_DH:ATT_DH + 1])
    return jnp.concatenate(outs_t, axis=0).T.astype(BF16)


def _band_prompt_kernel(rel_ref, q_ref, k_ref, vt_ref, o_ref, bias_ref):
    _, tq, nk = bias_ref.shape
    n_blocks = q_ref.shape[0] // tq
    n_past = ATT_PAST // tq

    @pl.when(pl.program_id(1) == 0)
    def _():
        _build_bias(rel_ref, bias_ref)

    vhs_t = _head_values_t(vt_ref[...])
    for tt in range(n_blocks):
        rows = slice(tt * tq, (tt + 1) * tq)
        keys = slice(max(tt - n_past, 0) * tq, (tt + 1) * tq)
        o_ref[rows, :] = _attend(q_ref[rows, :], k_ref[keys, :], [vh[:, keys] for vh in vhs_t], bias_ref,
                                 max(n_past - tt, 0) * tq)


def _band_prompt(p, v_t, rel, batch, t):
    w = ATT_LANES
    cq, ck = COL_QB // w, COL_KB // w
    return pl.pallas_call(
        _band_prompt_kernel,
        grid=(ATT_HEADS // ATT_GROUP, batch),
        in_specs=[
            pl.BlockSpec((ATT_GROUP, 1, REL_RING), lambda g, b: (g, 0, 0)),
            pl.BlockSpec((t, w), lambda g, b: (b, cq + g)),
            pl.BlockSpec((t, w), lambda g, b: (b, ck + g)),
            pl.BlockSpec((w, t), lambda g, b: (g, b)),
        ],
        out_specs=pl.BlockSpec((t, w), lambda g, b: (b, g)),
        out_shape=jax.ShapeDtypeStruct((batch * t, ATT_WIDTH), BF16),
        scratch_shapes=[pltpu.VMEM((ATT_GROUP, ATT_TQ, ATT_PAST + ATT_TQ), F32)],
        compiler_params=_params("arbitrary", "arbitrary"),
        name="band_prompt",
    )(rel, p, p, v_t)


def _attend_cached(q, kc_t, vc_t, k_new, v_new, bias):
    tq, c = q.shape[0], kc_t.shape[1]
    nt = (((1,), (1,)), ((), ()))
    q = q.astype(F32) * (ATT_DH ** -0.5 * LOG2E)
    qs = jnp.concatenate([jnp.where(_head_lanes(hh), q, 0.0) for hh in range(ATT_GROUP)], axis=0).astype(BF16)
    s = jnp.concatenate([jnp.dot(qs, kc_t, preferred_element_type=F32),
                         lax.dot_general(qs, k_new, nt, preferred_element_type=F32)], axis=1) + bias
    e = jnp.exp2(s - jnp.max(s, axis=-1, keepdims=True))
    eb = e.astype(BF16)
    pv = (lax.dot_general(eb[:, :c], vc_t, nt, preferred_element_type=F32)
          + jnp.dot(eb[:, c:], v_new, preferred_element_type=F32)) / jnp.sum(e, axis=-1, keepdims=True)
    out = pv[:tq]
    for hh in range(1, ATT_GROUP):
        out = jnp.where(_head_lanes(hh), pv[hh * tq:(hh + 1) * tq], out)
    return out.astype(BF16)


def _band_step_kernel(rel_ref, q_ref, k_ref, v_ref, ck_ref, cv_ref, o_ref, bias_ref):
    @pl.when(pl.program_id(0) == 0)
    def _():
        _build_bias(rel_ref, bias_ref)

    t, nk = bias_ref.shape[1:]
    c = ck_ref.shape[2]
    kc_t = ck_ref[...].reshape(ATT_WIDTH, c).astype(BF16)
    vc_t = cv_ref[...].reshape(ATT_WIDTH, c).astype(BF16)
    for g in range(ATT_HEADS // ATT_GROUP):
        lanes = slice(g * ATT_LANES, (g + 1) * ATT_LANES)
        bias = bias_ref[g * ATT_GROUP:(g + 1) * ATT_GROUP].reshape(ATT_GROUP * t, nk)
        o_ref[:, lanes] = _attend_cached(q_ref[:, lanes], kc_t[lanes, :], vc_t[lanes, :], k_ref[:, lanes],
                                         v_ref[:, lanes], bias)


def _band_step(p, cache_k_t, cache_v_t, rel, batch, t, row0):
    c = cache_k_t.shape[4]
    assert row0 % t == 0
    r0 = row0 // t
    new = lambda col: pl.BlockSpec((t, ATT_WIDTH), lambda b: (r0 + b, col // ATT_WIDTH))
    cache = pl.BlockSpec((None, None, ATT_HEADS, ATT_DH, c), lambda b: (0, b, 0, 0, 0))
    return pl.pallas_call(
        _band_step_kernel,
        grid=(batch,),
        in_specs=[_const_spec(rel.shape), new(COL_QB), new(COL_KB), new(COL_VB), cache, cache],
        out_specs=pl.BlockSpec((t, ATT_WIDTH), lambda b: (b, 0)),
        out_shape=jax.ShapeDtypeStruct((batch * t, ATT_WIDTH), BF16),
        scratch_shapes=[pltpu.VMEM((ATT_HEADS, t, c + t), F32)],
        compiler_params=_params("arbitrary"),
        name="band_step",
    )(rel, p, p, p, cache_k_t, cache_v_t)


SQUARE_STEPS = 8


def _mix_ffn2_kernel(h_ref, oap_ref, obp_ref, oas_ref, obs_ref, ga_ref, gb_ref, wbg_ref, wba_ref, wo_ref,
                     g2_ref, win_ref, wout_ref, gf_ref, yp_ref, ys_ref, wbg_s, wba_s, wo_s, win_s, wout_s,
                     *, n_prompt):
    step = pl.program_id(0)
    _stage(step, wbg_ref, wbg_s, 0)
    _stage(step, wba_ref, wba_s, 0)
    _stage(step, wo_ref, wo_s, 0)
    _stage(step, win_ref, win_s, 1)
    _stage(step, wout_ref, wout_s, 0)

    def tile(oa_ref, ob_ref, y_ref):
        a = jnp.dot(oa_ref[...], wbg_s[...], preferred_element_type=F32)
        b = jnp.dot(ob_ref[...], wba_s[...], preferred_element_type=F32)
        mixed = (jax.nn.sigmoid(ga_ref[...].astype(F32)) * a
                 + jax.nn.sigmoid(gb_ref[...].astype(F32)) * b)
        h = h_ref[...] + jnp.dot(mixed.astype(BF16), wo_s[...], preferred_element_type=F32)
        xn = _rms(h, g2_ref[...]).astype(BF16)
        h = h + 0.5 * _swiglu(xn, win_s, wout_s)
        y_ref[...] = _rms(h, gf_ref[...])

    @pl.when((step >= W_STEPS) & (step < W_STEPS + n_prompt))
    def _():
        tile(oap_ref, obp_ref, yp_ref)

    @pl.when(step >= W_STEPS + n_prompt)
    def _():
        tile(oas_ref, obs_ref, ys_ref)


def _mix_ffn2(h, oa_p, ob_p, oa_s, ob_s, p, wbg, wba, wo, g2, win, wout, gf):
    tiles = h.shape[0] // ROW_TILE
    n_prompt, n_sample = oa_p.shape[0] // ROW_TILE, oa_s.shape[0] // ROW_TILE
    assert tiles == n_prompt + n_sample
    once = dict(pipeline_mode=pl.Buffered(1))
    prompt, sample = _row_tiles(0, n_prompt), _row_tiles(n_prompt, n_sample, **once)
    square = _slab_spec(wbg.shape, 0, SQUARE_STEPS, **once)
    return pl.pallas_call(
        functools.partial(_mix_ffn2_kernel, n_prompt=n_prompt),
        grid=(W_STEPS + tiles,),
        in_specs=[_row_tiles(0, tiles), prompt, prompt, sample, sample,
                  _row_tiles(0, tiles, col=COL_GA // D_MODEL), _row_tiles(0, tiles, col=COL_GB // D_MODEL),
                  square, square, square, _const_spec(g2.shape), _slab_spec(win.shape, 1, FFN_SLABS),
                  _slab_spec(wout.shape, 0, FFN_SLABS), _const_spec(gf.shape)],
        out_specs=[prompt, sample],
        out_shape=[jax.ShapeDtypeStruct((n_prompt * ROW_TILE, D_MODEL), F32),
                   jax.ShapeDtypeStruct((n_sample * ROW_TILE, D_MODEL), F32)],
        scratch_shapes=[pltpu.VMEM(wbg.shape, BF16), pltpu.VMEM(wba.shape, BF16), pltpu.VMEM(wo.shape, BF16),
                        pltpu.VMEM(win.shape, BF16), pltpu.VMEM(wout.shape, BF16)],
        compiler_params=_params("arbitrary"),
        name="mix_ffn2",
    )(h, oa_p, ob_p, oa_s, ob_s, p, p, wbg, wba, wo, g2, win, wout, gf)


def _kept_rows(p, row0, batch, t, col):
    keep = min(ATT_PAST, t)
    rows = p[row0:row0 + batch * t, col:col + ATT_WIDTH].reshape(batch, t, ATT_WIDTH)[:, t - keep:]
    return rows.astype(F32).reshape(1, batch, keep, ATT_HEADS, ATT_DH)


def kernel(x_prompt, x_sample, cache_att_k, cache_att_v, state_gla, norm_ffn1, w_ffn1_in, w_ffn1_out,
           norm_mix, w_in, w_gla_gate, b_gla_gate, gla_norm, attn_rel_bias, w_branch_gla, w_branch_att,
           w_out, norm_ffn2, w_ffn2_in, w_ffn2_out, norm_final):
    assert norm_ffn1.shape[0] == 1, "single layer"
    bp, tp, _ = x_prompt.shape
    bs, ts, _ = x_sample.shape
    mp, ms = bp * tp, bs * ts
    assert mp % ROW_TILE == 0 and ms % ROW_TILE == 0
    assert cache_att_k.shape[2] == ATT_PAST and ts <= ATT_TQ

    wg = jnp.concatenate([w_gla_gate[0], jnp.zeros((LANES - GLA_GATE_RANK, GLA_K_WIDTH), F32)],
                         axis=0).astype(BF16)
    rel = _rel_row(attn_rel_bias[0])

    h, n = _ffn1(x_prompt.reshape(mp, D_MODEL), x_sample.reshape(ms, D_MODEL), norm_ffn1, w_ffn1_in[0],
                 w_ffn1_out[0], norm_mix)
    p, v_t, k_prompt_t, v_prompt_t = _proj(n, w_in[0].T, bp, tp)
    to_rows_last = lambda a: jnp.transpose(a, (0, 1, 3, 4, 2))
    to_rows_first = lambda a: jnp.transpose(a, (0, 1, 4, 2, 3))
    gla_w = (wg, b_gla_gate, gla_norm)
    oa_p, s_prompt = _gla(p, *gla_w, jnp.zeros((bp, GLA_HEADS, GLA_DK, GLA_DV), F32), bp, tp, 0)
    oa_s, s_sample = _gla(p, *gla_w, state_gla[0], bs, ts, mp)
    ob_p = _band_prompt(p, v_t, rel, bp, tp)
    ob_s = _band_step(p, to_rows_last(cache_att_k), to_rows_last(cache_att_v), rel, bs, ts, mp)
    yp, ys = _mix_ffn2(h, oa_p, ob_p, oa_s, ob_s, p, w_branch_gla[0], w_branch_att[0], w_out[0], norm_ffn2,
                       w_ffn2_in[0], w_ffn2_out[0], norm_final.reshape(1, D_MODEL))
    return (yp.reshape(bp, tp, D_MODEL), ys.reshape(bs, ts, D_MODEL), to_rows_first(k_prompt_t),
            to_rows_first(v_prompt_t), s_prompt[None],
            _kept_rows(p, mp, bs, ts, COL_KB), _kept_rows(p, mp, bs, ts, COL_VB), s_sample[None])
```
